```python
import math
import jax
import jax.numpy as jnp
from jax import lax
import numpy as np


D_MODEL = 1024
BATCH = 8
SEQ = 4096
DEPTH = 2

CTX_LEN = 256
GRID_W = 64
EPS = 1e-6
F32 = jnp.float32

CONV_WIDTH = 512
SSM_WIDTH = 512
SSM_GROUP = 16
SSM_GROUPS = SSM_WIDTH // SSM_GROUP
SSM_STATE = 64
DT_MIN = 1e-3
DT_MAX = 1e-1
EVEN_IN = 4 * CONV_WIDTH + 2 * SSM_WIDTH
EVEN_MIX = CONV_WIDTH + SSM_WIDTH

MLA_HEADS = 16
QK_NOPE = 64
QK_ROPE = 32
QK_DIM = QK_NOPE + QK_ROPE
V_HEAD = 64
Q_LORA = 384
KV_LORA = 256
MLA_MIX = MLA_HEADS * V_HEAD
ODD_IN = Q_LORA + KV_LORA + QK_ROPE + MLA_MIX
ROPE_BASE = 10000.0
Q_BLOCK = 128

N_EVEN = (DEPTH + 1) // 2
N_ODD = DEPTH // 2

kernel_name = 'hybrid_conv_s5_mla_prefix_dit'


def rms_norm(x, w):
    xf = x.astype(F32)
    y = xf * lax.rsqrt(jnp.mean(xf * xf, axis=-1, keepdims=True) + EPS)
    return (y * w.astype(F32)).astype(x.dtype)


def modulation(cond, w, b):
    m = jax.nn.silu(cond) @ w + b
    return jnp.split(m, 3, axis=-1)


def short_conv(v, w):
    n = v.shape[1]
    vp = jnp.pad(v, ((0, 0), (1, 1), (0, 0)))
    return vp[:, :n] * w[0] + vp[:, 1:n + 1] * w[1] + vp[:, 2:] * w[2]


def s5_discretize(lam_re, lam_im, log_step, b_re, b_im):
    lr = lam_re.astype(F32)
    li = lam_im.astype(F32)
    dt = jnp.exp(log_step.astype(F32))[:, None]
    mag = jnp.exp(lr * dt)
    ar = mag * jnp.cos(li * dt)
    ai = mag * jnp.sin(li * dt)
    nr = ar - 1.0
    den = lr * lr + li * li
    fr = (nr * lr + ai * li) / den
    fi = (ai * lr - nr * li) / den
    br = b_re.astype(F32)
    bi = b_im.astype(F32)
    bbr = fr[..., None] * br - fi[..., None] * bi
    bbi = fr[..., None] * bi + fi[..., None] * br
    return ar, ai, bbr, bbi


def _complex_affine_combine(e1, e2):
    a1r, a1i, b1r, b1i = e1
    a2r, a2i, b2r, b2i = e2
    return (a1r * a2r - a1i * a2i,
            a1r * a2i + a1i * a2r,
            a2r * b1r - a2i * b1i + b2r,
            a2r * b1i + a2i * b1r + b2i)


def s5_scan(u_t, lbr, lbi, bbr, bbi, h0_re, h0_im):
    bu_re = jnp.einsum('lbgc,gpc->lbgp', u_t, bbr)
    bu_im = jnp.einsum('lbgc,gpc->lbgp', u_t, bbi)
    if h0_re is not None:
        bu_re = bu_re.at[0].add(lbr * h0_re - lbi * h0_im)
        bu_im = bu_im.at[0].add(lbr * h0_im + lbi * h0_re)
    n = u_t.shape[0]
    a_re = jnp.broadcast_to(lbr, (n, 1) + lbr.shape)
    a_im = jnp.broadcast_to(lbi, (n, 1) + lbi.shape)
    _, _, s_re, s_im = lax.associative_scan(_complex_affine_combine, (a_re, a_im, bu_re, bu_im), axis=0)
    return s_re, s_im


def s5_readout(s_re, s_im, c_re, c_im):
    return jnp.einsum('lbgp,gcp->lbgc', s_re, c_re) - jnp.einsum('lbgp,gcp->lbgc', s_im, c_im)


def to_time_groups(u):
    b_, n, _ = u.shape
    return jnp.transpose(u.reshape(b_, n, SSM_GROUPS, SSM_GROUP), (1, 0, 2, 3)).astype(F32)


def s5_glu(y_t, u_t, d_skip, glu_w, glu_b):
    n, b_, g_, cg = y_t.shape
    y = y_t + d_skip.astype(F32).reshape(g_, cg) * u_t
    y = jnp.transpose(y, (1, 0, 2, 3)).reshape(b_, n, g_ * cg)
    g = jax.nn.gelu(y)
    return g * jax.nn.sigmoid(g @ glu_w.astype(F32) + glu_b.astype(F32))


def s5_branch(u_ctx, u_lat, lam_re, lam_im, log_step, b_re, b_im, c_re, c_im, d_skip, glu_w, glu_b, need_ctx):
    uc = to_time_groups(u_ctx)
    ul = to_time_groups(u_lat)
    ys_lat = []
    ys_ctx = []
    for d in range(2):
        rev = d == 1
        lbr, lbi, bbr, bbi = s5_discretize(lam_re[d], lam_im[d], log_step[d], b_re[d], b_im[d])
        cr = c_re[d].astype(F32)
        ci = c_im[d].astype(F32)
        ucd = jnp.flip(uc, 0) if rev else uc
        uld = jnp.flip(ul, 0) if rev else ul
        sc_re, sc_im = s5_scan(ucd, lbr, lbi, bbr, bbi, None, None)
        sl_re, sl_im = s5_scan(uld, lbr, lbi, bbr, bbi, sc_re[-1], sc_im[-1])
        yl = s5_readout(sl_re, sl_im, cr, ci)
        ys_lat.append(jnp.flip(yl, 0) if rev else yl)
        if need_ctx:
            yc = s5_readout(sc_re, sc_im, cr, ci)
            ys_ctx.append(jnp.flip(yc, 0) if rev else yc)
    y_lat = s5_glu(ys_lat[0] + ys_lat[1], ul, d_skip, glu_w, glu_b)
    y_ctx = s5_glu(ys_ctx[0] + ys_ctx[1], uc, d_skip, glu_w, glu_b) if need_ctx else None
    return y_lat, y_ctx


def even_mixer(a_lat, a_ctx, w_in, conv_w, lam_re, lam_im, log_step, b_re, b_im, c_re, c_im,
               d_skip, glu_w, glu_b, w_out, need_ctx):
    cuts = [CONV_WIDTH, 2 * CONV_WIDTH, 3 * CONV_WIDTH, 4 * CONV_WIDTH, 4 * CONV_WIDTH + SSM_WIDTH]

    def conv_branch(xa, ba, ca, za):
        return ba * short_conv(ca * xa, conv_w) * jax.nn.silu(za)

    xa_l, ba_l, ca_l, za_l, us_l, zs_l = jnp.split(a_lat @ w_in, cuts, axis=-1)
    xa_c, ba_c, ca_c, za_c, us_c, zs_c = jnp.split(a_ctx @ w_in, cuts, axis=-1)
    ssm_l, ssm_c = s5_branch(us_c, us_l, lam_re, lam_im, log_step, b_re, b_im, c_re, c_im,
                             d_skip, glu_w, glu_b, need_ctx)
    mix_l = jnp.concatenate([conv_branch(xa_l, ba_l, ca_l, za_l).astype(F32),
                             ssm_l * jax.nn.silu(zs_l.astype(F32))], axis=-1)
    out_l = mix_l @ w_out.astype(F32)
    out_c = None
    if need_ctx:
        mix_c = jnp.concatenate([conv_branch(xa_c, ba_c, ca_c, za_c).astype(F32),
                                 ssm_c * jax.nn.silu(zs_c.astype(F32))], axis=-1)
        out_c = mix_c @ w_out.astype(F32)
    return out_l, out_c


def axial_rope_tables(rows):
    row = jnp.repeat(jnp.arange(rows, dtype=F32), GRID_W)
    col = jnp.tile(jnp.arange(GRID_W, dtype=F32), rows)
    n_freq = QK_ROPE // 4
    inv = jnp.power(ROPE_BASE, -jnp.arange(n_freq, dtype=F32) / n_freq)
    ang = jnp.concatenate([row[:, None] * inv, col[:, None] * inv], axis=-1)
    return jnp.cos(ang), jnp.sin(ang)


def apply_rope(x, cos, sin):
    half = QK_ROPE // 2
    x1 = x[..., :half]
    x2 = x[..., half:]
    return jnp.concatenate([x1 * cos - x2 * sin, x1 * sin + x2 * cos], axis=-1).astype(x.dtype)


def block_attention(q, k, v):
    b_, s_, h_, dq = q.shape
    nb = s_ // Q_BLOCK
    qb = jnp.transpose(q.reshape(b_, nb, Q_BLOCK, h_, dq), (1, 0, 2, 3, 4))
    scale = dq ** -0.5

    def one_block(qblk):
        s = jnp.einsum('bqhd,bkhd->bhqk', qblk, k, preferred_element_type=F32) * scale
        p = jax.nn.softmax(s, axis=-1)
        return jnp.einsum('bhqk,bkhv->bqhv', p.astype(v.dtype), v)

    out = lax.map(one_block, qb)
    return jnp.transpose(out, (1, 0, 2, 3, 4)).reshape(b_, s_, h_, v.shape[-1])


def mla_mixer(a_lat, a_ctx, w_in, q_a_norm, w_uq, kv_a_norm, w_ukv, q_norm, k_norm, w_out, rope, need_ctx):
    cuts = [Q_LORA, Q_LORA + KV_LORA, Q_LORA + KV_LORA + QK_ROPE]

    def project(h, rope_tab):
        b_, n, _ = h.shape
        cq, ckv, kr, gate = jnp.split(h @ w_in, cuts, axis=-1)
        q = (rms_norm(cq, q_a_norm) @ w_uq).reshape(b_, n, MLA_HEADS, QK_DIM)
        kv = (rms_norm(ckv, kv_a_norm) @ w_ukv).reshape(b_, n, MLA_HEADS, QK_NOPE + V_HEAD)
        q_nope = rms_norm(q[..., :QK_NOPE], q_norm[:QK_NOPE])
        q_rope = rms_norm(q[..., QK_NOPE:], q_norm[QK_NOPE:])
        k_nope = rms_norm(kv[..., :QK_NOPE], k_norm[:QK_NOPE])
        v = kv[..., QK_NOPE:]
        k_rope = rms_norm(kr, k_norm[QK_NOPE:])
        if rope_tab is not None:
            cos, sin = rope_tab
            q_rope = apply_rope(q_rope, cos[:, None, :], sin[:, None, :])
            k_rope = apply_rope(k_rope, cos, sin)
        q = jnp.concatenate([q_nope, q_rope], axis=-1)
        k = jnp.concatenate([k_nope, jnp.broadcast_to(k_rope[:, :, None, :], (b_, n, MLA_HEADS, QK_ROPE))], axis=-1)
        return q, k, v, gate

    q_l, k_l, v_l, g_l = project(a_lat, rope)
    q_c, k_c, v_c, g_c = project(a_ctx, None)
    b_, n, _ = a_lat.shape
    att_l = block_attention(q_l, jnp.concatenate([k_c, k_l], axis=1), jnp.concatenate([v_c, v_l], axis=1))
    out_l = (att_l.reshape(b_, n, MLA_MIX) * jax.nn.silu(g_l)) @ w_out
    out_c = None
    if need_ctx:
        att_c = block_attention(q_c, k_c, v_c)
        out_c = (att_c.reshape(b_, a_ctx.shape[1], MLA_MIX) * jax.nn.silu(g_c)) @ w_out
    return out_l, out_c


def setup_inputs(seed: int = 0) -> dict:
    key = jax.random.key(seed)
    ks = iter(jax.random.split(key, 32))
    nrm = lambda shape, s: jax.random.normal(next(ks), shape, F32) * s
    gain = lambda shape: 1.0 + 0.02 * jax.random.normal(next(ks), shape, F32)
    lam_im_base = jnp.pi * jnp.arange(SSM_STATE, dtype=F32)
    return {
        'x': nrm((BATCH, SEQ, D_MODEL), 1.0),
        'c': nrm((BATCH, D_MODEL), 1.0),
        'ctx': nrm((BATCH, CTX_LEN, D_MODEL), 1.0),
        'c_ctx': nrm((D_MODEL,), 1.0),
        'mod_w': nrm((DEPTH, D_MODEL, 3 * D_MODEL), 0.5 * D_MODEL ** -0.5),
        'mod_b': nrm((DEPTH, 3 * D_MODEL), 0.02),
        'norm_w': gain((DEPTH, D_MODEL)),
        'e_w_in': nrm((N_EVEN, D_MODEL, EVEN_IN), D_MODEL ** -0.5),
        'e_conv_w': nrm((N_EVEN, 3, CONV_WIDTH), 3 ** -0.5),
        'e_lam_re': -0.5 + nrm((N_EVEN, 2, SSM_GROUPS, SSM_STATE), 0.01),
        'e_lam_im': lam_im_base + nrm((N_EVEN, 2, SSM_GROUPS, SSM_STATE), 0.01),
        'e_log_step': jax.random.uniform(next(ks), (N_EVEN, 2, SSM_GROUPS), F32,
                                         minval=math.log(DT_MIN), maxval=math.log(DT_MAX)),
        'e_b_re': nrm((N_EVEN, 2, SSM_GROUPS, SSM_STATE, SSM_GROUP), (2 * SSM_GROUP) ** -0.5),
        'e_b_im': nrm((N_EVEN, 2, SSM_GROUPS, SSM_STATE, SSM_GROUP), (2 * SSM_GROUP) ** -0.5),
        'e_c_re': nrm((N_EVEN, 2, SSM_GROUPS, SSM_GROUP, SSM_STATE), SSM_STATE ** -0.5),
        'e_c_im': nrm((N_EVEN, 2, SSM_GROUPS, SSM_GROUP, SSM_STATE), SSM_STATE ** -0.5),
        'e_d': nrm((N_EVEN, SSM_WIDTH), 1.0),
        'e_glu_w': nrm((N_EVEN, SSM_WIDTH, SSM_WIDTH), SSM_WIDTH ** -0.5),
        'e_glu_b': nrm((N_EVEN, SSM_WIDTH), 0.02),
        'e_w_out': nrm((N_EVEN, EVEN_MIX, D_MODEL), EVEN_MIX ** -0.5),
        'o_w_in': nrm((N_ODD, D_MODEL, ODD_IN), D_MODEL ** -0.5),
        'o_q_a_norm': gain((N_ODD, Q_LORA)),
        'o_w_uq': nrm((N_ODD, Q_LORA, MLA_HEADS * QK_DIM), Q_LORA ** -0.5),
        'o_kv_a_norm': gain((N_ODD, KV_LORA)),
        'o_w_ukv': nrm((N_ODD, KV_LORA, MLA_HEADS * (QK_NOPE + V_HEAD)), KV_LORA ** -0.5),
        'o_q_norm': gain((N_ODD, QK_DIM)),
        'o_k_norm': gain((N_ODD, QK_DIM)),
        'o_w_out': nrm((N_ODD, MLA_MIX, D_MODEL), MLA_MIX ** -0.5),
    }


def reference(x, c, ctx, c_ctx, mod_w, mod_b, norm_w,
              e_w_in, e_conv_w, e_lam_re, e_lam_im, e_log_step, e_b_re, e_b_im, e_c_re, e_c_im,
              e_d, e_glu_w, e_glu_b, e_w_out,
              o_w_in, o_q_a_norm, o_w_uq, o_kv_a_norm, o_w_ukv, o_q_norm, o_k_norm, o_w_out):
    rows = x.shape[1] // GRID_W
    rope = axial_rope_tables(rows)
    h = x
    hc = ctx
    for i in range(DEPTH):
        need_ctx = i < DEPTH - 1
        shift, scale, gate = modulation(c, mod_w[i], mod_b[i])
        shift_c, scale_c, gate_c = modulation(c_ctx, mod_w[i], mod_b[i])
        a_lat = rms_norm(h, norm_w[i]) * (1 + scale[:, None]) + shift[:, None]
        a_ctx = rms_norm(hc, norm_w[i]) * (1 + scale_c) + shift_c
        if i % 2 == 0:
            j = i // 2
            out_l, out_c = even_mixer(a_lat, a_ctx, e_w_in[j], e_conv_w[j], e_lam_re[j], e_lam_im[j],
                                      e_log_step[j], e_b_re[j], e_b_im[j], e_c_re[j], e_c_im[j],
                                      e_d[j], e_glu_w[j], e_glu_b[j], e_w_out[j], need_ctx)
        else:
            j = i // 2
            out_l, out_c = mla_mixer(a_lat, a_ctx, o_w_in[j], o_q_a_norm[j], o_w_uq[j], o_kv_a_norm[j],
                                     o_w_ukv[j], o_q_norm[j], o_k_norm[j], o_w_out[j], rope, need_ctx)
        h = h + (gate[:, None] * out_l).astype(h.dtype)
        if need_ctx:
            hc = hc + (gate_c * out_c).astype(hc.dtype)
    return h
```

```python
import functools
import math

import jax
import jax.numpy as jnp
from jax import lax
from jax.experimental import pallas as pl
from jax.experimental.pallas import tpu as pltpu

F32 = jnp.float32
BF16 = jnp.bfloat16

D_MODEL = 1024
BATCH = 8
SEQ = 4096
CTX_LEN = 256
GRID_W = 64
EPS = 1e-6
CONV_WIDTH = 512
SSM_WIDTH = 512
SSM_GROUP = 16
SSM_GROUPS = SSM_WIDTH // SSM_GROUP
SSM_STATE = 64
MLA_HEADS = 16
QK_NOPE = 64
QK_ROPE = 32
QK_DIM = QK_NOPE + QK_ROPE
V_HEAD = 64
Q_LORA = 384
KV_LORA = 256
MLA_MIX = MLA_HEADS * V_HEAD
ROPE_BASE = 10000.0

LANES = 128
SUBLANES = 8
VMEM_LIMIT_BYTES = 56 * 1024 * 1024

N_TOK = CTX_LEN + SEQ
TOK_BLOCK = 256
N_BLOCKS = N_TOK // TOK_BLOCK
CTX_BLOCKS = CTX_LEN // TOK_BLOCK
assert CTX_BLOCKS == 1 and N_TOK % TOK_BLOCK == 0
COND_ROWS = 2 * SUBLANES
CTX_ROW = BATCH
EVEN_IN = 4 * CONV_WIDTH + 2 * SSM_WIDTH

SCAN_STEPS = 64
SCAN_ROWS = SCAN_STEPS * BATCH
SCAN_CHUNKS = N_TOK // SCAN_STEPS
SCAN_CTX_CHUNKS = CTX_LEN // SCAN_STEPS
SLAB_GROUPS = LANES // SSM_GROUP
N_SLABS = SSM_GROUPS // SLAB_GROUPS
SLAB_STATE = SLAB_GROUPS * SSM_STATE
N_STATE = SSM_GROUPS * SSM_STATE

HEAD_PAD = LANES
Q_BLOCK = 256
HEADS_PER_STEP = 2
KV_CHUNK = 256
ODD_IN_PAD = Q_LORA + KV_LORA + LANES + MLA_MIX
SOFTMAX_SCALE = QK_DIM ** -0.5 * math.log2(math.e)


def _params(*sem):
    return pltpu.CompilerParams(dimension_semantics=sem, vmem_limit_bytes=VMEM_LIMIT_BYTES)


def _dot(a, b):
    return jnp.dot(a, b, preferred_element_type=F32)


def _split_bf16(a):
    hi = a.astype(BF16)
    lo = (a - hi.astype(F32)).astype(BF16)
    return hi, lo


def _dot3(a, b):
    ah, al = _split_bf16(a)
    bh, bl = _split_bf16(b)
    return _dot(ah, bh) + (_dot(ah, bl) + _dot(al, bh))


def _rms(x, w):
    return x * lax.rsqrt(jnp.mean(x * x, axis=-1, keepdims=True) + EPS) * w


def _modulated_norm(h, norm_w, mod):
    return _rms(h, norm_w) * (1.0 + mod[1:2, :]) + mod[0:1, :]


def _mod_kernel(cond_ref, w_ref, b_ref, o_ref):
    cond = cond_ref[...]
    o_ref[0] = _dot3(cond * jax.nn.sigmoid(cond), w_ref[0]) + b_ref[0]


def _modulation(cond, mod_w, mod_b):
    depth = mod_w.shape[0]
    n_col = 3 * D_MODEL // D_MODEL
    return pl.pallas_call(
        _mod_kernel,
        out_shape=jax.ShapeDtypeStruct((depth, COND_ROWS, 3 * D_MODEL), F32),
        grid=(depth, n_col),
        in_specs=[
            pl.BlockSpec((COND_ROWS, D_MODEL), lambda i, j: (0, 0)),
            pl.BlockSpec((1, D_MODEL, D_MODEL), lambda i, j: (i, 0, j)),
            pl.BlockSpec((1, 1, D_MODEL), lambda i, j: (i, 0, j)),
        ],
        out_specs=pl.BlockSpec((1, COND_ROWS, D_MODEL), lambda i, j: (i, 0, j)),
        compiler_params=_params("arbitrary", "arbitrary"),
        name="modulation",
    )(cond, mod_w, mod_b.reshape(depth, 1, 3 * D_MODEL))


def _ctx_map(b, t):
    return (b, 0, 0)


def _lat_map(b, t):
    return (b, jnp.maximum(t - CTX_BLOCKS, 0), 0)


def _tok_map(b, t):
    return (b, t, 0)


def _tm_map(b, t):
    return (t, b)


def _mod_map(b, t):
    return (jnp.where(t < CTX_BLOCKS, CTX_ROW, b), 0, 0)


def _const2(b, t):
    return (0, 0)


def _pick_hidden(ctx_ref, x_ref):
    is_ctx = pl.program_id(1) < CTX_BLOCKS
    return jnp.where(is_ctx, ctx_ref[0], x_ref[0])


def _even_in_kernel(ctx_ref, x_ref, mod_ref, nw_ref, w_ref, p_ref, q_ref, zs_ref, u_ref):
    h = _pick_hidden(ctx_ref, x_ref)
    a = _modulated_norm(h, nw_ref[...], mod_ref[0])
    r = _dot(a.astype(BF16), w_ref[...])
    cw = CONV_WIDTH
    xa, ba, ca, za = (r[:, i * cw:(i + 1) * cw] for i in range(4))
    us = r[:, 4 * cw:4 * cw + SSM_WIDTH]
    zs = r[:, 4 * cw + SSM_WIDTH:]
    p_ref[0] = ca * xa
    q_ref[0] = ba * (za * jax.nn.sigmoid(za))
    zs_ref[0] = zs * jax.nn.sigmoid(zs)
    u_ref[...] = us


def _even_in(ctx, x, mod, norm_w, w_in):
    tok = jax.ShapeDtypeStruct((BATCH, N_TOK, CONV_WIDTH), F32)
    return pl.pallas_call(
        _even_in_kernel,
        out_shape=(tok, tok, tok, jax.ShapeDtypeStruct((N_TOK, BATCH * SSM_WIDTH), F32)),
        grid=(BATCH, N_BLOCKS),
        in_specs=[
            pl.BlockSpec((1, TOK_BLOCK, D_MODEL), _ctx_map),
            pl.BlockSpec((1, TOK_BLOCK, D_MODEL), _lat_map),
            pl.BlockSpec((1, 3, D_MODEL), _mod_map),
            pl.BlockSpec((1, D_MODEL), _const2),
            pl.BlockSpec((D_MODEL, EVEN_IN), _const2),
        ],
        out_specs=(
            pl.BlockSpec((1, TOK_BLOCK, CONV_WIDTH), _tok_map),
            pl.BlockSpec((1, TOK_BLOCK, CONV_WIDTH), _tok_map),
            pl.BlockSpec((1, TOK_BLOCK, SSM_WIDTH), _tok_map),
            pl.BlockSpec((TOK_BLOCK, SSM_WIDTH), _tm_map),
        ),
        compiler_params=_params("arbitrary", "arbitrary"),
        name="even_in",
    )(ctx, x, mod, norm_w, w_in)


def _s5_disc_kernel(lr_ref, li_ref, ls_ref, br_ref, bi_ref, ar_ref, ai_ref, bbr_ref, bbi_ref):
    lr = lr_ref[...]
    li = li_ref[...]
    dt = jnp.exp(ls_ref[...])
    mag = jnp.exp(lr * dt)
    ar = mag * jnp.cos(li * dt)
    ai = mag * jnp.sin(li * dt)
    nr = ar - 1.0
    den = lr * lr + li * li
    fr = (nr * lr + ai * li) / den
    fi = (ai * lr - nr * li) / den
    ar_ref[...] = ar
    ai_ref[...] = ai
    br = br_ref[...]
    bi = bi_ref[...]
    bbr_ref[...] = fr[:, None, :] * br - fi[:, None, :] * bi
    bbi_ref[...] = fr[:, None, :] * bi + fi[:, None, :] * br


def _s5_discretize(lam_re, lam_im, log_step, b_re, b_im):
    flat = lambda a: a.reshape(2, N_STATE)
    chan = lambda a: jnp.transpose(a, (0, 3, 1, 2)).reshape(2, SSM_GROUP, N_STATE)
    ls = jnp.repeat(log_step, SSM_STATE, axis=-1)
    vec = jax.ShapeDtypeStruct((2, N_STATE), F32)
    mat = jax.ShapeDtypeStruct((2, SSM_GROUP, N_STATE), F32)
    return pl.pallas_call(_s5_disc_kernel, out_shape=(vec, vec, mat, mat), name="s5_discretize")(
        flat(lam_re), flat(lam_im), ls, chan(b_re), chan(b_im))


def _block_diag_groups(a):
    d, j, g, r, c = a.shape
    eye = jnp.eye(g, dtype=bool)[None, None, :, None, :, None]
    full = jnp.where(eye, a[:, :, :, :, None, :], jnp.zeros((), a.dtype))
    return full.reshape(d, j, g * r, g * c)


def _s5_matrices(bbr, bbi, c_re, c_im):
    def in_map(bb):
        a = bb.reshape(2, SSM_GROUP, N_SLABS, SLAB_GROUPS, SSM_STATE)
        return _block_diag_groups(jnp.transpose(a, (0, 2, 3, 1, 4)))
    w_b = jnp.concatenate([in_map(bbr), in_map(bbi)], axis=-1).astype(BF16)

    def out_map(cc):
        a = cc.reshape(2, N_SLABS, SLAB_GROUPS, SSM_GROUP, SSM_STATE)
        return _block_diag_groups(jnp.transpose(a, (0, 1, 2, 4, 3))).astype(BF16)
    return w_b, out_map(c_re), out_map(c_im)


def _scan_chunk(d, i):
    back = jnp.where(i < SCAN_CTX_CHUNKS, SCAN_CTX_CHUNKS - 1 - i,
                     SCAN_CHUNKS - 1 - (i - SCAN_CTX_CHUNKS))
    return jnp.where(d == 0, i, back)


def _s5_scan_kernel(u_ref, wb_ref, cre_ref, cim_ref, ar_ref, ai_ref, y_ref, bu_ref, st_ref):
    d = pl.program_id(0)

    @pl.when(pl.program_id(1) == 0)
    def _():
        st_ref[...] = jnp.zeros_like(st_ref)

    ub = u_ref[...].astype(BF16)
    width = 2 * SLAB_STATE
    for j in range(N_SLABS):
        bu_ref[:, j * width:(j + 1) * width] = _dot(ub[:, j * LANES:(j + 1) * LANES], wb_ref[0, j])

    n_piece = 2 * N_SLABS

    def step(k, state):
        t = jnp.where(d == 0, k, SCAN_STEPS - 1 - k)
        rows = pl.ds(pl.multiple_of(t * BATCH, BATCH), BATCH)
        new = []
        for j in range(N_SLABS):
            s_re, s_im = state[2 * j], state[2 * j + 1]
            a_re = ar_ref[0, :, j * SLAB_STATE:(j + 1) * SLAB_STATE]
            a_im = ai_ref[0, :, j * SLAB_STATE:(j + 1) * SLAB_STATE]
            re_cols = slice(j * width, j * width + SLAB_STATE)
            im_cols = slice(j * width + SLAB_STATE, (j + 1) * width)
            n_re = a_re * s_re - a_im * s_im + bu_ref[rows, re_cols]
            n_im = a_re * s_im + a_im * s_re + bu_ref[rows, im_cols]
            bu_ref[rows, re_cols] = n_re
            bu_ref[rows, im_cols] = n_im
            new += [n_re, n_im]
        return tuple(new)

    init = tuple(st_ref[:, p * SLAB_STATE:(p + 1) * SLAB_STATE] for p in range(n_piece))
    final = lax.fori_loop(0, SCAN_STEPS, step, init, unroll=2)
    for p in range(n_piece):
        st_ref[:, p * SLAB_STATE:(p + 1) * SLAB_STATE] = final[p]

    for j in range(N_SLABS):
        s_re = bu_ref[:, j * width:j * width + SLAB_STATE].astype(BF16)
        s_im = bu_ref[:, j * width + SLAB_STATE:(j + 1) * width].astype(BF16)
        y_ref[0, :, j * LANES:(j + 1) * LANES] = _dot(s_re, cre_ref[0, j]) - _dot(s_im, cim_ref[0, j])


def _s5_scan(u_tm, w_b, c_re, c_im, a_re, a_im):
    dir_map = lambda d, i: (d, 0, 0, 0)
    vec_map = lambda d, i: (d, 0, 0)
    return pl.pallas_call(
        _s5_scan_kernel,
        out_shape=jax.ShapeDtypeStruct((2, N_TOK * BATCH, SSM_WIDTH), F32),
        grid=(2, SCAN_CHUNKS),
        in_specs=[
            pl.BlockSpec((SCAN_ROWS, SSM_WIDTH), lambda d, i: (_scan_chunk(d, i), 0)),
            pl.BlockSpec((1, N_SLABS, LANES, 2 * SLAB_STATE), dir_map),
            pl.BlockSpec((1, N_SLABS, SLAB_STATE, LANES), dir_map),
            pl.BlockSpec((1, N_SLABS, SLAB_STATE, LANES), dir_map),
            pl.BlockSpec((1, BATCH, N_STATE), vec_map),
            pl.BlockSpec((1, BATCH, N_STATE), vec_map),
        ],
        out_specs=pl.BlockSpec((1, SCAN_ROWS, SSM_WIDTH), lambda d, i: (d, _scan_chunk(d, i), 0)),
        scratch_shapes=[
            pltpu.VMEM((SCAN_ROWS, 2 * N_STATE), F32),
            pltpu.VMEM((BATCH, 2 * N_STATE), F32),
        ],
        compiler_params=_params("arbitrary", "arbitrary"),
        name="s5_scan",
    )(u_tm, w_b, c_re, c_im, a_re, a_im)


HALO_BLOCKS = TOK_BLOCK // SUBLANES


def _even_out_kernel(y_ref, u_ref, p_ref, pprev_ref, pnext_ref, q_ref, zs_ref, ctx_ref, x_ref, mod_ref,
                     d_ref, gw_ref, gb_ref, cw_ref, wo_ref, o_ref):
    t = pl.program_id(1)
    u = u_ref[...]
    y = y_ref[0] + y_ref[1] + d_ref[...] * u
    g = jax.nn.gelu(y)
    z = _dot(g.astype(BF16), gw_ref[...]) + gb_ref[...]
    ssm = g * jax.nn.sigmoid(z) * zs_ref[0]

    p = p_ref[0]
    row = lax.broadcasted_iota(jnp.int32, p.shape, 0)
    prev_ok = (t > CTX_BLOCKS).astype(F32)
    next_ok = jnp.logical_and(t >= CTX_BLOCKS, t < N_BLOCKS - 1).astype(F32)
    halo_prev = pprev_ref[0, SUBLANES - 1:SUBLANES, :] * prev_ok
    halo_next = pnext_ref[0, 0:1, :] * next_ok
    p_prev = jnp.where(row == 0, halo_prev, pltpu.roll(p, 1, 0))
    p_next = jnp.where(row == TOK_BLOCK - 1, halo_next, pltpu.roll(p, TOK_BLOCK - 1, 0))
    cw = cw_ref[...]
    conv = q_ref[0] * (p_prev * cw[0:1, :] + p * cw[1:2, :] + p_next * cw[2:3, :])

    out = _dot(conv.astype(BF16), wo_ref[0:CONV_WIDTH, :]) + _dot(ssm.astype(BF16), wo_ref[CONV_WIDTH:, :])
    o_ref[0] = _pick_hidden(ctx_ref, x_ref) + mod_ref[0, 2:3, :] * out


def _even_out(y_tm, u_tm, p, q, zs, ctx, x, mod, d_skip, glu_w, glu_b, conv_w, w_out):
    halo_prev = lambda b, t: (b, jnp.maximum(t * HALO_BLOCKS - 1, 0), 0)
    halo_next = lambda b, t: (b, jnp.minimum((t + 1) * HALO_BLOCKS, N_TOK // SUBLANES - 1), 0)
    return pl.pallas_call(
        _even_out_kernel,
        out_shape=jax.ShapeDtypeStruct((BATCH, N_TOK, D_MODEL), F32),
        grid=(BATCH, N_BLOCKS),
        in_specs=[
            pl.BlockSpec((2, TOK_BLOCK, SSM_WIDTH), lambda b, t: (0, t, b)),
            pl.BlockSpec((TOK_BLOCK, SSM_WIDTH), _tm_map),
            pl.BlockSpec((1, TOK_BLOCK, CONV_WIDTH), _tok_map),
            pl.BlockSpec((1, SUBLANES, CONV_WIDTH), halo_prev),
            pl.BlockSpec((1, SUBLANES, CONV_WIDTH), halo_next),
            pl.BlockSpec((1, TOK_BLOCK, CONV_WIDTH), _tok_map),
            pl.BlockSpec((1, TOK_BLOCK, SSM_WIDTH), _tok_map),
            pl.BlockSpec((1, TOK_BLOCK, D_MODEL), _ctx_map),
            pl.BlockSpec((1, TOK_BLOCK, D_MODEL), _lat_map),
            pl.BlockSpec((1, 3, D_MODEL), _mod_map),
            pl.BlockSpec((1, SSM_WIDTH), _const2),
            pl.BlockSpec((SSM_WIDTH, SSM_WIDTH), _const2),
            pl.BlockSpec((1, SSM_WIDTH), _const2),
            pl.BlockSpec((3, CONV_WIDTH), _const2),
            pl.BlockSpec((CONV_WIDTH + SSM_WIDTH, D_MODEL), _const2),
        ],
        out_specs=pl.BlockSpec((1, TOK_BLOCK, D_MODEL), _tok_map),
        compiler_params=_params("arbitrary", "arbitrary"),
        name="even_out",
    )(y_tm, u_tm, p, p, p, q, zs, ctx, x, mod, d_skip, glu_w, glu_b, conv_w, w_out)


def _rope(v, cos_t, sin_lo, sin_hi):
    half = QK_ROPE // 2
    return v * cos_t + pltpu.roll(v, LANES - half, 1) * sin_lo + pltpu.roll(v, half, 1) * sin_hi


def _mla_in_kernel(h_ref, mod_ref, nw_ref, win_ref, qan_ref, wuq_ref, kvan_ref, wukv_ref,
                   gq_ref, gkn_ref, gkr_ref, cos_ref, slo_ref, shi_ref,
                   q_ref, k_ref, v_ref, gate_ref):
    a = _modulated_norm(h_ref[0], nw_ref[...], mod_ref[0])
    r = _dot(a.astype(BF16), win_ref[...])
    cq = r[:, :Q_LORA]
    ckv = r[:, Q_LORA:Q_LORA + KV_LORA]
    kr = r[:, Q_LORA + KV_LORA:Q_LORA + KV_LORA + LANES]
    gate = r[:, Q_LORA + KV_LORA + LANES:]
    gate_ref[0] = gate * jax.nn.sigmoid(gate)

    qf = _dot(_rms(cq, qan_ref[...]).astype(BF16), wuq_ref[...])
    kvf = _dot(_rms(ckv, kvan_ref[...]).astype(BF16), wukv_ref[...])

    cos_t, sin_lo, sin_hi = cos_ref[...], slo_ref[...], shi_ref[...]
    kr_ms = jnp.sum(kr * kr, axis=-1, keepdims=True) * (1.0 / QK_ROPE)
    k_rope = _rope(kr * lax.rsqrt(kr_ms + EPS) * gkr_ref[...], cos_t, sin_lo, sin_hi)

    lane = lax.broadcasted_iota(jnp.int32, (TOK_BLOCK, HEAD_PAD), 1)
    is_nope = lane < QK_NOPE
    for h in range(MLA_HEADS):
        cols = slice(h * HEAD_PAD, (h + 1) * HEAD_PAD)
        qh = qf[:, cols]
        sq = qh * qh
        ms_nope = jnp.sum(jnp.where(is_nope, sq, 0.0), axis=-1, keepdims=True) * (1.0 / QK_NOPE)
        ms_rope = jnp.sum(jnp.where(is_nope, 0.0, sq), axis=-1, keepdims=True) * (1.0 / QK_ROPE)
        inv = jnp.where(is_nope, lax.rsqrt(ms_nope + EPS), lax.rsqrt(ms_rope + EPS))
        qn = _rope(qh * inv * gq_ref[...], cos_t, sin_lo, sin_hi)
        q_ref[0, h] = (qn * SOFTMAX_SCALE).astype(BF16)

        kh = kvf[:, cols]
        k_ms = jnp.sum(kh * kh, axis=-1, keepdims=True) * (1.0 / QK_NOPE)
        k_ref[0, h] = (kh * lax.rsqrt(k_ms + EPS) * gkn_ref[...] + k_rope).astype(BF16)
        v_ref[0, h] = kvf[:, MLA_HEADS * HEAD_PAD + h * HEAD_PAD:MLA_HEADS * HEAD_PAD + (h + 1) * HEAD_PAD].astype(BF16)


def _mla_in(h, mod, norm_w, w_in, q_a_norm, w_uq, kv_a_norm, w_ukv, gq, gkn, gkr, cos_t, sin_lo, sin_hi):
    head = jax.ShapeDtypeStruct((BATCH, MLA_HEADS, N_TOK, HEAD_PAD), BF16)
    head_spec = pl.BlockSpec((1, MLA_HEADS, TOK_BLOCK, HEAD_PAD), lambda b, t: (b, 0, t, 0))
    tab_spec = pl.BlockSpec((TOK_BLOCK, HEAD_PAD), lambda b, t: (t, 0))
    row = lambda n: pl.BlockSpec((1, n), _const2)
    return pl.pallas_call(
        _mla_in_kernel,
        out_shape=(head, head, head, jax.ShapeDtypeStruct((BATCH, N_TOK, MLA_MIX), F32)),
        grid=(BATCH, N_BLOCKS),
        in_specs=[
            pl.BlockSpec((1, TOK_BLOCK, D_MODEL), _tok_map),
            pl.BlockSpec((1, 3, D_MODEL), _mod_map),
            row(D_MODEL),
            pl.BlockSpec((D_MODEL, ODD_IN_PAD), _const2),
            row(Q_LORA),
            pl.BlockSpec((Q_LORA, MLA_HEADS * HEAD_PAD), _const2),
            row(KV_LORA),
            pl.BlockSpec((KV_LORA, 2 * MLA_HEADS * HEAD_PAD), _const2),
            row(HEAD_PAD), row(HEAD_PAD), row(HEAD_PAD),
            tab_spec, tab_spec, tab_spec,
        ],
        out_specs=(head_spec, head_spec, head_spec, pl.BlockSpec((1, TOK_BLOCK, MLA_MIX), _tok_map)),
        compiler_params=_params("arbitrary", "arbitrary"),
        name="mla_in",
    )(h, mod, norm_w, w_in, q_a_norm, w_uq, kv_a_norm, w_ukv, gq, gkn, gkr, cos_t, sin_lo, sin_hi)


def _attention_kernel(q_ref, k_ref, v_ref, o_ref, s_ref):
    out = None
    for hh in range(HEADS_PER_STEP):
        s_ref[...] = lax.dot_general(q_ref[0, hh], k_ref[0, hh], (((1,), (1,)), ((), ())),
                                     preferred_element_type=F32)
        mx = s_ref[:, 0:LANES]
        for j in range(1, N_TOK // LANES):
            mx = jnp.maximum(mx, s_ref[:, j * LANES:(j + 1) * LANES])
        m = jnp.max(mx, axis=-1, keepdims=True)
        acc = jnp.zeros((Q_BLOCK, HEAD_PAD), F32)
        part = jnp.zeros((Q_BLOCK, LANES), F32)
        for c in range(N_TOK // KV_CHUNK):
            keys = slice(c * KV_CHUNK, (c + 1) * KV_CHUNK)
            pc = jnp.exp2(s_ref[:, keys] - m)
            for j in range(KV_CHUNK // LANES):
                part = part + pc[:, j * LANES:(j + 1) * LANES]
            acc = acc + _dot(pc.astype(BF16), v_ref[0, hh, keys, :])
        o_h = acc / jnp.sum(part, axis=-1, keepdims=True)
        out = o_h if out is None else out + o_h
    o_ref[0] = out


def _attention(q, k, v):
    pairs = MLA_HEADS // HEADS_PER_STEP
    q_off = CTX_LEN // Q_BLOCK
    kv_spec = pl.BlockSpec((1, HEADS_PER_STEP, N_TOK, HEAD_PAD), lambda b, hp, i: (b, hp, 0, 0))
    return pl.pallas_call(
        _attention_kernel,
        out_shape=jax.ShapeDtypeStruct((BATCH, SEQ, MLA_MIX), F32),
        grid=(BATCH, pairs, SEQ // Q_BLOCK),
        in_specs=[
            pl.BlockSpec((1, HEADS_PER_STEP, Q_BLOCK, HEAD_PAD), lambda b, hp, i: (b, hp, i + q_off, 0)),
            kv_spec, kv_spec,
        ],
        out_specs=pl.BlockSpec((1, Q_BLOCK, HEADS_PER_STEP * V_HEAD), lambda b, hp, i: (b, i, hp)),
        scratch_shapes=[pltpu.VMEM((Q_BLOCK, N_TOK), F32)],
        compiler_params=_params("arbitrary", "arbitrary", "arbitrary"),
        name="attention",
    )(q, k, v)


def _mla_out_kernel(att_ref, gate_ref, h_ref, mod_ref, wo_ref, o_ref):
    mix = (att_ref[0] * gate_ref[0]).astype(BF16)
    o_ref[0] = h_ref[0] + mod_ref[0, 2:3, :] * _dot(mix, wo_ref[...])


def _mla_out(att, gate, h, mod, w_out):
    lat_tok = lambda b, t: (b, t + CTX_BLOCKS, 0)
    return pl.pallas_call(
        _mla_out_kernel,
        out_shape=jax.ShapeDtypeStruct((BATCH, SEQ, D_MODEL), F32),
        grid=(BATCH, SEQ // TOK_BLOCK),
        in_specs=[
            pl.BlockSpec((1, TOK_BLOCK, MLA_MIX), _tok_map),
            pl.BlockSpec((1, TOK_BLOCK, MLA_MIX), lat_tok),
            pl.BlockSpec((1, TOK_BLOCK, D_MODEL), lat_tok),
            pl.BlockSpec((1, 3, D_MODEL), lambda b, t: (b, 0, 0)),
            pl.BlockSpec((MLA_MIX, D_MODEL), _const2),
        ],
        out_specs=pl.BlockSpec((1, TOK_BLOCK, D_MODEL), _tok_map),
        compiler_params=_params("arbitrary", "arbitrary"),
        name="mla_out",
    )(att, gate, h, mod, w_out)


def _pad_heads(w, width, offset=0):
    k = w.shape[0]
    w = w.reshape(k, MLA_HEADS, width)
    return jnp.pad(w, ((0, 0), (0, 0), (offset, HEAD_PAD - width - offset))).reshape(k, MLA_HEADS * HEAD_PAD)


def _mla_weights(w_in, w_uq, w_ukv, q_norm, k_norm):
    cuts = [Q_LORA, Q_LORA + KV_LORA, Q_LORA + KV_LORA + QK_ROPE]
    w_kr = jnp.pad(w_in[:, cuts[1]:cuts[2]], ((0, 0), (QK_NOPE, LANES - QK_DIM)))
    w_in_p = jnp.concatenate([w_in[:, :cuts[1]], w_kr, w_in[:, cuts[2]:]], axis=1).astype(BF16)
    w_uq_p = _pad_heads(w_uq, QK_DIM).astype(BF16)
    w_kv = w_ukv.reshape(KV_LORA, MLA_HEADS, QK_NOPE + V_HEAD)
    w_k = _pad_heads(w_kv[:, :, :QK_NOPE].reshape(KV_LORA, -1), QK_NOPE)
    w_v = w_kv[:, :, QK_NOPE:].reshape(KV_LORA, MLA_HEADS // 2, 2, V_HEAD)
    w_v = jnp.stack([jnp.pad(w_v[:, :, 0], ((0, 0), (0, 0), (0, HEAD_PAD - V_HEAD))),
                     jnp.pad(w_v[:, :, 1], ((0, 0), (0, 0), (V_HEAD, HEAD_PAD - 2 * V_HEAD)))], axis=2)
    w_ukv_p = jnp.concatenate([w_k, w_v.reshape(KV_LORA, MLA_HEADS * HEAD_PAD)], axis=1).astype(BF16)
    pad_row = lambda v, off: jnp.pad(v, (off, HEAD_PAD - v.shape[0] - off)).reshape(1, HEAD_PAD)
    gq = pad_row(q_norm, 0)
    gkn = pad_row(k_norm[:QK_NOPE], 0)
    gkr = pad_row(k_norm[QK_NOPE:], QK_NOPE)
    return w_in_p, w_uq_p, w_ukv_p, gq, gkn, gkr


def _rope_tables():
    rows = SEQ // GRID_W
    row = jnp.repeat(jnp.arange(rows, dtype=F32), GRID_W)
    col = jnp.tile(jnp.arange(GRID_W, dtype=F32), rows)
    n_freq = QK_ROPE // 4
    inv = jnp.power(ROPE_BASE, -jnp.arange(n_freq, dtype=F32) / n_freq)
    ang = jnp.concatenate([row[:, None] * inv, col[:, None] * inv], axis=-1)
    cos, sin = jnp.cos(ang), jnp.sin(ang)
    half = QK_ROPE // 2
    zeros = jnp.zeros_like(cos)
    pad_l = jnp.ones((SEQ, QK_NOPE), F32)
    tail = jnp.zeros((SEQ, HEAD_PAD - QK_DIM), F32)
    cos_t = jnp.concatenate([pad_l, cos, cos, tail], axis=-1)
    sin_lo = jnp.concatenate([0 * pad_l, -sin, zeros, tail], axis=-1)
    sin_hi = jnp.concatenate([0 * pad_l, zeros, sin, tail], axis=-1)
    ctx_cos = jnp.concatenate([jnp.ones((CTX_LEN, QK_DIM), F32), jnp.zeros((CTX_LEN, HEAD_PAD - QK_DIM), F32)], -1)
    ctx_zero = jnp.zeros((CTX_LEN, HEAD_PAD), F32)
    del half
    return (jnp.concatenate([ctx_cos, cos_t], 0), jnp.concatenate([ctx_zero, sin_lo], 0),
            jnp.concatenate([ctx_zero, sin_hi], 0))


def kernel(x, c, ctx, c_ctx, mod_w, mod_b, norm_w, e_w_in, e_conv_w, e_lam_re, e_lam_im, e_log_step, e_b_re, e_b_im, e_c_re, e_c_im, e_d, e_glu_w, e_glu_b, e_w_out, o_w_in, o_q_a_norm, o_w_uq, o_kv_a_norm, o_w_ukv, o_q_norm, o_k_norm, o_w_out):
    cond = jnp.zeros((COND_ROWS, D_MODEL), F32).at[:BATCH].set(c).at[CTX_ROW].set(c_ctx)
    mod = _modulation(cond, mod_w, mod_b)[:, :CTX_ROW + 1].reshape(2, CTX_ROW + 1, 3, D_MODEL)

    p, q, zs, u_tm = _even_in(ctx, x, mod[0], norm_w[0:1], e_w_in[0].astype(BF16))
    a_re, a_im, bbr, bbi = _s5_discretize(e_lam_re[0], e_lam_im[0], e_log_step[0], e_b_re[0], e_b_im[0])
    w_b, c_re, c_im = _s5_matrices(bbr, bbi, e_c_re[0], e_c_im[0])
    bcast = lambda a: jnp.broadcast_to(a[:, None, :], (2, BATCH, N_STATE))
    y = _s5_scan(u_tm.reshape(N_TOK * BATCH, SSM_WIDTH), w_b, c_re, c_im, bcast(a_re), bcast(a_im))
    h1 = _even_out(y.reshape(2, N_TOK, BATCH * SSM_WIDTH), u_tm, p, q, zs, ctx, x, mod[0],
                   e_d[0:1], e_glu_w[0].astype(BF16), e_glu_b[0:1], e_conv_w[0], e_w_out[0].astype(BF16))

    w_in_p, w_uq_p, w_ukv_p, gq, gkn, gkr = _mla_weights(o_w_in[0], o_w_uq[0], o_w_ukv[0], o_q_norm[0], o_k_norm[0])
    cos_t, sin_lo, sin_hi = _rope_tables()
    qh, kh, vh, gate = _mla_in(h1, mod[1], norm_w[1:2], w_in_p, o_q_a_norm[0:1], w_uq_p, o_kv_a_norm[0:1],
                               w_ukv_p, gq, gkn, gkr, cos_t, sin_lo, sin_hi)
    att = _attention(qh, kh, vh)
    return _mla_out(att, gate, h1, mod[1], o_w_out[0].astype(BF16))
```

```python
import functools
import math

import jax
import jax.numpy as jnp
from jax import lax
from jax.experimental import pallas as pl
from jax.experimental.pallas import tpu as pltpu

F32 = jnp.float32
BF16 = jnp.bfloat16

D_MODEL = 1024
BATCH = 8
SEQ = 4096
CTX_LEN = 256
GRID_W = 64
EPS = 1e-6
CONV_WIDTH = 512
SSM_WIDTH = 512
SSM_GROUP = 16
SSM_GROUPS = SSM_WIDTH // SSM_GROUP
SSM_STATE = 64
MLA_HEADS = 16
QK_NOPE = 64
QK_ROPE = 32
QK_DIM = QK_NOPE + QK_ROPE
V_HEAD = 64
Q_LORA = 384
KV_LORA = 256
MLA_MIX = MLA_HEADS * V_HEAD
ROPE_BASE = 10000.0

LANES = 128
SUBLANES = 8
VMEM_LIMIT_BYTES = 56 * 1024 * 1024

N_TOK = CTX_LEN + SEQ
TOK_BLOCK = 256
N_BLOCKS = N_TOK // TOK_BLOCK
CTX_BLOCKS = CTX_LEN // TOK_BLOCK
assert CTX_BLOCKS == 1 and N_TOK % TOK_BLOCK == 0
COND_ROWS = 2 * SUBLANES
CTX_ROW = BATCH
EVEN_IN = 4 * CONV_WIDTH + 2 * SSM_WIDTH

SCAN_STEPS = 64
SCAN_ROWS = SCAN_STEPS * BATCH
SCAN_CHUNKS = N_TOK // SCAN_STEPS
SCAN_CTX_CHUNKS = CTX_LEN // SCAN_STEPS
SLAB_GROUPS = LANES // SSM_GROUP
N_SLABS = SSM_GROUPS // SLAB_GROUPS
SLAB_STATE = SLAB_GROUPS * SSM_STATE
N_STATE = SSM_GROUPS * SSM_STATE

HEAD_PAD = LANES
Q_BLOCK = 512
HEADS_PER_STEP = 2
KV_CHUNK = 256
assert KV_CHUNK == TOK_BLOCK
ODD_IN_PAD = Q_LORA + KV_LORA + LANES + MLA_MIX
SOFTMAX_SCALE = QK_DIM ** -0.5 * math.log2(math.e)


def _params(*sem):
    return pltpu.CompilerParams(dimension_semantics=sem, vmem_limit_bytes=VMEM_LIMIT_BYTES)


def _dot(a, b):
    return jnp.dot(a, b, preferred_element_type=F32)


def _split_bf16(a):
    hi = a.astype(BF16)
    lo = (a - hi.astype(F32)).astype(BF16)
    return hi, lo


def _dot3(a, b):
    ah, al = _split_bf16(a)
    bh, bl = _split_bf16(b)
    return _dot(ah, bh) + (_dot(ah, bl) + _dot(al, bh))


def _rms(x, w):
    return x * lax.rsqrt(jnp.mean(x * x, axis=-1, keepdims=True) + EPS) * w


def _modulated_norm(h, norm_w, mod):
    return _rms(h, norm_w) * (1.0 + mod[1:2, :]) + mod[0:1, :]


def _mod_kernel(cond_ref, w_ref, b_ref, o_ref):
    cond = cond_ref[...]
    o_ref[0] = _dot3(cond * jax.nn.sigmoid(cond), w_ref[0]) + b_ref[0]


def _modulation(cond, mod_w, mod_b):
    depth = mod_w.shape[0]
    n_col = 3 * D_MODEL // D_MODEL
    return pl.pallas_call(
        _mod_kernel,
        out_shape=jax.ShapeDtypeStruct((depth, COND_ROWS, 3 * D_MODEL), F32),
        grid=(depth, n_col),
        in_specs=[
            pl.BlockSpec((COND_ROWS, D_MODEL), lambda i, j: (0, 0)),
            pl.BlockSpec((1, D_MODEL, D_MODEL), lambda i, j: (i, 0, j)),
            pl.BlockSpec((1, 1, D_MODEL), lambda i, j: (i, 0, j)),
        ],
        out_specs=pl.BlockSpec((1, COND_ROWS, D_MODEL), lambda i, j: (i, 0, j)),
        compiler_params=_params("arbitrary", "arbitrary"),
        name="modulation",
    )(cond, mod_w, mod_b.reshape(depth, 1, 3 * D_MODEL))


def _ctx_map(b, t):
    return (b, 0, 0)


def _lat_map(b, t):
    return (b, jnp.maximum(t - CTX_BLOCKS, 0), 0)


def _tok_map(b, t):
    return (b, t, 0)


def _tm_map(b, t):
    return (t, b)


def _mod_map(b, t):
    return (jnp.where(t < CTX_BLOCKS, CTX_ROW, b), 0, 0)


def _const2(b, t):
    return (0, 0)


def _pick_hidden(ctx_ref, x_ref):
    is_ctx = pl.program_id(1) < CTX_BLOCKS
    return jnp.where(is_ctx, ctx_ref[0], x_ref[0])


def _even_in_kernel(ctx_ref, x_ref, mod_ref, nw_ref, w_ref, p_ref, q_ref, zs_ref, u_ref):
    h = _pick_hidden(ctx_ref, x_ref)
    a = _modulated_norm(h, nw_ref[...], mod_ref[0])
    r = _dot(a.astype(BF16), w_ref[...])
    cw = CONV_WIDTH
    xa, ba, ca, za = (r[:, i * cw:(i + 1) * cw] for i in range(4))
    us = r[:, 4 * cw:4 * cw + SSM_WIDTH]
    zs = r[:, 4 * cw + SSM_WIDTH:]
    p_ref[0] = ca * xa
    q_ref[0] = ba * (za * jax.nn.sigmoid(za))
    zs_ref[0] = zs * jax.nn.sigmoid(zs)
    u_ref[...] = us


def _even_in(ctx, x, mod, norm_w, w_in):
    tok = jax.ShapeDtypeStruct((BATCH, N_TOK, CONV_WIDTH), F32)
    return pl.pallas_call(
        _even_in_kernel,
        out_shape=(tok, tok, tok, jax.ShapeDtypeStruct((N_TOK, BATCH * SSM_WIDTH), F32)),
        grid=(BATCH, N_BLOCKS),
        in_specs=[
            pl.BlockSpec((1, TOK_BLOCK, D_MODEL), _ctx_map),
            pl.BlockSpec((1, TOK_BLOCK, D_MODEL), _lat_map),
            pl.BlockSpec((1, 3, D_MODEL), _mod_map),
            pl.BlockSpec((1, D_MODEL), _const2),
            pl.BlockSpec((D_MODEL, EVEN_IN), _const2),
        ],
        out_specs=(
            pl.BlockSpec((1, TOK_BLOCK, CONV_WIDTH), _tok_map),
            pl.BlockSpec((1, TOK_BLOCK, CONV_WIDTH), _tok_map),
            pl.BlockSpec((1, TOK_BLOCK, SSM_WIDTH), _tok_map),
            pl.BlockSpec((TOK_BLOCK, SSM_WIDTH), _tm_map),
        ),
        compiler_params=_params("arbitrary", "arbitrary"),
        name="even_in",
    )(ctx, x, mod, norm_w, w_in)


def _s5_disc_kernel(lr_ref, li_ref, ls_ref, br_ref, bi_ref, ar_ref, ai_ref, bbr_ref, bbi_ref):
    lr = lr_ref[...]
    li = li_ref[...]
    dt = jnp.exp(ls_ref[...])
    mag = jnp.exp(lr * dt)
    ar = mag * jnp.cos(li * dt)
    ai = mag * jnp.sin(li * dt)
    nr = ar - 1.0
    den = lr * lr + li * li
    fr = (nr * lr + ai * li) / den
    fi = (ai * lr - nr * li) / den
    ar_ref[...] = ar
    ai_ref[...] = ai
    br = br_ref[...]
    bi = bi_ref[...]
    bbr_ref[...] = fr[:, None, :] * br - fi[:, None, :] * bi
    bbi_ref[...] = fr[:, None, :] * bi + fi[:, None, :] * br


def _s5_discretize(lam_re, lam_im, log_step, b_re, b_im):
    flat = lambda a: a.reshape(2, N_STATE)
    chan = lambda a: jnp.transpose(a, (0, 3, 1, 2)).reshape(2, SSM_GROUP, N_STATE)
    ls = jnp.repeat(log_step, SSM_STATE, axis=-1)
    vec = jax.ShapeDtypeStruct((2, N_STATE), F32)
    mat = jax.ShapeDtypeStruct((2, SSM_GROUP, N_STATE), F32)
    return pl.pallas_call(_s5_disc_kernel, out_shape=(vec, vec, mat, mat), name="s5_discretize")(
        flat(lam_re), flat(lam_im), ls, chan(b_re), chan(b_im))


def _block_diag_groups(a):
    d, j, g, r, c = a.shape
    eye = jnp.eye(g, dtype=bool)[None, None, :, None, :, None]
    full = jnp.where(eye, a[:, :, :, :, None, :], jnp.zeros((), a.dtype))
    return full.reshape(d, j, g * r, g * c)


def _s5_matrices(bbr, bbi, c_re, c_im):
    def in_map(bb):
        a = bb.reshape(2, SSM_GROUP, N_SLABS, SLAB_GROUPS, SSM_STATE)
        return _block_diag_groups(jnp.transpose(a, (0, 2, 3, 1, 4)))
    w_b = jnp.concatenate([in_map(bbr), in_map(bbi)], axis=-1).astype(BF16)

    def out_map(cc):
        a = cc.reshape(2, N_SLABS, SLAB_GROUPS, SSM_GROUP, SSM_STATE)
        return _block_diag_groups(jnp.transpose(a, (0, 1, 2, 4, 3))).astype(BF16)
    return w_b, out_map(c_re), out_map(c_im)


def _scan_chunk(d, i):
    back = jnp.where(i < SCAN_CTX_CHUNKS, SCAN_CTX_CHUNKS - 1 - i,
                     SCAN_CHUNKS - 1 - (i - SCAN_CTX_CHUNKS))
    return jnp.where(d == 0, i, back)


def _s5_scan_kernel(u_ref, wb_ref, cre_ref, cim_ref, ar_ref, ai_ref, y_ref, bu_ref, st_ref):
    d = pl.program_id(0)

    @pl.when(pl.program_id(1) == 0)
    def _():
        st_ref[...] = jnp.zeros_like(st_ref)

    ub = u_ref[...].astype(BF16)
    width = 2 * SLAB_STATE
    for j in range(N_SLABS):
        bu_ref[:, j * width:(j + 1) * width] = _dot(ub[:, j * LANES:(j + 1) * LANES], wb_ref[0, j])

    n_piece = 2 * N_SLABS

    def step(k, state):
        t = jnp.where(d == 0, k, SCAN_STEPS - 1 - k)
        rows = pl.ds(pl.multiple_of(t * BATCH, BATCH), BATCH)
        new = []
        for j in range(N_SLABS):
            s_re, s_im = state[2 * j], state[2 * j + 1]
            a_re = ar_ref[0, :, j * SLAB_STATE:(j + 1) * SLAB_STATE]
            a_im = ai_ref[0, :, j * SLAB_STATE:(j + 1) * SLAB_STATE]
            re_cols = slice(j * width, j * width + SLAB_STATE)
            im_cols = slice(j * width + SLAB_STATE, (j + 1) * width)
            n_re = a_re * s_re - a_im * s_im + bu_ref[rows, re_cols]
            n_im = a_re * s_im + a_im * s_re + bu_ref[rows, im_cols]
            bu_ref[rows, re_cols] = n_re
            bu_ref[rows, im_cols] = n_im
            new += [n_re, n_im]
        return tuple(new)

    init = tuple(st_ref[:, p * SLAB_STATE:(p + 1) * SLAB_STATE] for p in range(n_piece))
    final = lax.fori_loop(0, SCAN_STEPS, step, init, unroll=2)
    for p in range(n_piece):
        st_ref[:, p * SLAB_STATE:(p + 1) * SLAB_STATE] = final[p]

    for j in range(N_SLABS):
        s_re = bu_ref[:, j * width:j * width + SLAB_STATE].astype(BF16)
        s_im = bu_ref[:, j * width + SLAB_STATE:(j + 1) * width].astype(BF16)
        y_ref[0, :, j * LANES:(j + 1) * LANES] = _dot(s_re, cre_ref[0, j]) - _dot(s_im, cim_ref[0, j])


def _s5_scan(u_tm, w_b, c_re, c_im, a_re, a_im):
    dir_map = lambda d, i: (d, 0, 0, 0)
    vec_map = lambda d, i: (d, 0, 0)
    return pl.pallas_call(
        _s5_scan_kernel,
        out_shape=jax.ShapeDtypeStruct((2, N_TOK * BATCH, SSM_WIDTH), F32),
        grid=(2, SCAN_CHUNKS),
        in_specs=[
            pl.BlockSpec((SCAN_ROWS, SSM_WIDTH), lambda d, i: (_scan_chunk(d, i), 0)),
            pl.BlockSpec((1, N_SLABS, LANES, 2 * SLAB_STATE), dir_map),
            pl.BlockSpec((1, N_SLABS, SLAB_STATE, LANES), dir_map),
            pl.BlockSpec((1, N_SLABS, SLAB_STATE, LANES), dir_map),
            pl.BlockSpec((1, BATCH, N_STATE), vec_map),
            pl.BlockSpec((1, BATCH, N_STATE), vec_map),
        ],
        out_specs=pl.BlockSpec((1, SCAN_ROWS, SSM_WIDTH), lambda d, i: (d, _scan_chunk(d, i), 0)),
        scratch_shapes=[
            pltpu.VMEM((SCAN_ROWS, 2 * N_STATE), F32),
            pltpu.VMEM((BATCH, 2 * N_STATE), F32),
        ],
        compiler_params=_params("arbitrary", "arbitrary"),
        name="s5_scan",
    )(u_tm, w_b, c_re, c_im, a_re, a_im)


HALO_BLOCKS = TOK_BLOCK // SUBLANES


def _even_out_kernel(y_ref, u_ref, p_ref, pprev_ref, pnext_ref, q_ref, zs_ref, ctx_ref, x_ref, mod_ref,
                     d_ref, gw_ref, gb_ref, cw_ref, wo_ref, o_ref):
    t = pl.program_id(1)
    u = u_ref[...]
    y = y_ref[0] + y_ref[1] + d_ref[...] * u
    g = jax.nn.gelu(y)
    z = _dot(g.astype(BF16), gw_ref[...]) + gb_ref[...]
    ssm = g * jax.nn.sigmoid(z) * zs_ref[0]

    p = p_ref[0]
    row = lax.broadcasted_iota(jnp.int32, p.shape, 0)
    prev_ok = (t > CTX_BLOCKS).astype(F32)
    next_ok = jnp.logical_and(t >= CTX_BLOCKS, t < N_BLOCKS - 1).astype(F32)
    halo_prev = pprev_ref[0, SUBLANES - 1:SUBLANES, :] * prev_ok
    halo_next = pnext_ref[0, 0:1, :] * next_ok
    p_prev = jnp.where(row == 0, halo_prev, pltpu.roll(p, 1, 0))
    p_next = jnp.where(row == TOK_BLOCK - 1, halo_next, pltpu.roll(p, TOK_BLOCK - 1, 0))
    cw = cw_ref[...]
    conv = q_ref[0] * (p_prev * cw[0:1, :] + p * cw[1:2, :] + p_next * cw[2:3, :])

    out = _dot(conv.astype(BF16), wo_ref[0:CONV_WIDTH, :]) + _dot(ssm.astype(BF16), wo_ref[CONV_WIDTH:, :])
    o_ref[0] = _pick_hidden(ctx_ref, x_ref) + mod_ref[0, 2:3, :] * out


def _even_out(y_tm, u_tm, p, q, zs, ctx, x, mod, d_skip, glu_w, glu_b, conv_w, w_out):
    halo_prev = lambda b, t: (b, jnp.maximum(t * HALO_BLOCKS - 1, 0), 0)
    halo_next = lambda b, t: (b, jnp.minimum((t + 1) * HALO_BLOCKS, N_TOK // SUBLANES - 1), 0)
    return pl.pallas_call(
        _even_out_kernel,
        out_shape=jax.ShapeDtypeStruct((BATCH, N_TOK, D_MODEL), F32),
        grid=(BATCH, N_BLOCKS),
        in_specs=[
            pl.BlockSpec((2, TOK_BLOCK, SSM_WIDTH), lambda b, t: (0, t, b)),
            pl.BlockSpec((TOK_BLOCK, SSM_WIDTH), _tm_map),
            pl.BlockSpec((1, TOK_BLOCK, CONV_WIDTH), _tok_map),
            pl.BlockSpec((1, SUBLANES, CONV_WIDTH), halo_prev),
            pl.BlockSpec((1, SUBLANES, CONV_WIDTH), halo_next),
            pl.BlockSpec((1, TOK_BLOCK, CONV_WIDTH), _tok_map),
            pl.BlockSpec((1, TOK_BLOCK, SSM_WIDTH), _tok_map),
            pl.BlockSpec((1, TOK_BLOCK, D_MODEL), _ctx_map),
            pl.BlockSpec((1, TOK_BLOCK, D_MODEL), _lat_map),
            pl.BlockSpec((1, 3, D_MODEL), _mod_map),
            pl.BlockSpec((1, SSM_WIDTH), _const2),
            pl.BlockSpec((SSM_WIDTH, SSM_WIDTH), _const2),
            pl.BlockSpec((1, SSM_WIDTH), _const2),
            pl.BlockSpec((3, CONV_WIDTH), _const2),
            pl.BlockSpec((CONV_WIDTH + SSM_WIDTH, D_MODEL), _const2),
        ],
        out_specs=pl.BlockSpec((1, TOK_BLOCK, D_MODEL), _tok_map),
        compiler_params=_params("arbitrary", "arbitrary"),
        name="even_out",
    )(y_tm, u_tm, p, p, p, q, zs, ctx, x, mod, d_skip, glu_w, glu_b, conv_w, w_out)


def _rope(v, cos_t, sin_lo, sin_hi):
    half = QK_ROPE // 2
    return v * cos_t + pltpu.roll(v, LANES - half, 1) * sin_lo + pltpu.roll(v, half, 1) * sin_hi


def _mla_in_kernel(h_ref, mod_ref, nw_ref, win_ref, qan_ref, wuq_ref, kvan_ref, wukv_ref,
                   gq_ref, gkn_ref, gkr_ref, cos_ref, slo_ref, shi_ref,
                   q_ref, k_ref, vt_ref, gate_ref):
    a = _modulated_norm(h_ref[0], nw_ref[...], mod_ref[0])
    r = _dot(a.astype(BF16), win_ref[...])
    cq = r[:, :Q_LORA]
    ckv = r[:, Q_LORA:Q_LORA + KV_LORA]
    kr = r[:, Q_LORA + KV_LORA:Q_LORA + KV_LORA + LANES]
    gate = r[:, Q_LORA + KV_LORA + LANES:]
    gate_ref[0] = gate * jax.nn.sigmoid(gate)

    qf = _dot(_rms(cq, qan_ref[...]).astype(BF16), wuq_ref[...])
    kvf = _dot(_rms(ckv, kvan_ref[...]).astype(BF16), wukv_ref[...])

    cos_t, sin_lo, sin_hi = cos_ref[...], slo_ref[...], shi_ref[...]
    kr_ms = jnp.sum(kr * kr, axis=-1, keepdims=True) * (1.0 / QK_ROPE)
    k_rope = _rope(kr * lax.rsqrt(kr_ms + EPS) * gkr_ref[...], cos_t, sin_lo, sin_hi)

    lane = lax.broadcasted_iota(jnp.int32, (TOK_BLOCK, HEAD_PAD), 1)
    is_nope = lane < QK_NOPE
    for h in range(MLA_HEADS):
        cols = slice(h * HEAD_PAD, (h + 1) * HEAD_PAD)
        qh = qf[:, cols]
        sq = qh * qh
        ms_nope = jnp.sum(jnp.where(is_nope, sq, 0.0), axis=-1, keepdims=True) * (1.0 / QK_NOPE)
        ms_rope = jnp.sum(jnp.where(is_nope, 0.0, sq), axis=-1, keepdims=True) * (1.0 / QK_ROPE)
        inv = jnp.where(is_nope, lax.rsqrt(ms_nope + EPS), lax.rsqrt(ms_rope + EPS))
        qn = _rope(qh * inv * gq_ref[...], cos_t, sin_lo, sin_hi)
        q_ref[0, h] = (qn * SOFTMAX_SCALE).astype(BF16)

        kh = kvf[:, cols]
        k_ms = jnp.sum(kh * kh, axis=-1, keepdims=True) * (1.0 / QK_NOPE)
        k_ref[0, h] = (kh * lax.rsqrt(k_ms + EPS) * gkn_ref[...] + k_rope).astype(BF16)
    vt = kvf[:, MLA_HEADS * HEAD_PAD:].T.astype(BF16)
    vt_ref[0, :, 0] = vt.reshape(MLA_HEADS, V_HEAD, TOK_BLOCK)


def _mla_in(h, mod, norm_w, w_in, q_a_norm, w_uq, kv_a_norm, w_ukv, gq, gkn, gkr, cos_t, sin_lo, sin_hi):
    head = jax.ShapeDtypeStruct((BATCH, MLA_HEADS, N_TOK, HEAD_PAD), BF16)
    head_spec = pl.BlockSpec((1, MLA_HEADS, TOK_BLOCK, HEAD_PAD), lambda b, t: (b, 0, t, 0))
    q_lat = jax.ShapeDtypeStruct((BATCH, MLA_HEADS, SEQ, HEAD_PAD), BF16)
    q_spec = pl.BlockSpec((1, MLA_HEADS, TOK_BLOCK, HEAD_PAD),
                          lambda b, t: (b, 0, jnp.maximum(t - CTX_BLOCKS, 0), 0))
    vt = jax.ShapeDtypeStruct((BATCH, MLA_HEADS, N_TOK // KV_CHUNK, V_HEAD, KV_CHUNK), BF16)
    vt_spec = pl.BlockSpec((1, MLA_HEADS, 1, V_HEAD, KV_CHUNK), lambda b, t: (b, 0, t, 0, 0))
    tab_spec = pl.BlockSpec((TOK_BLOCK, HEAD_PAD), lambda b, t: (t, 0))
    row = lambda n: pl.BlockSpec((1, n), _const2)
    return pl.pallas_call(
        _mla_in_kernel,
        out_shape=(q_lat, head, vt, jax.ShapeDtypeStruct((BATCH, N_TOK, MLA_MIX), F32)),
        grid=(BATCH, N_BLOCKS),
        in_specs=[
            pl.BlockSpec((1, TOK_BLOCK, D_MODEL), _tok_map),
            pl.BlockSpec((1, 3, D_MODEL), _mod_map),
            row(D_MODEL),
            pl.BlockSpec((D_MODEL, ODD_IN_PAD), _const2),
            row(Q_LORA),
            pl.BlockSpec((Q_LORA, MLA_HEADS * HEAD_PAD), _const2),
            row(KV_LORA),
            pl.BlockSpec((KV_LORA, MLA_HEADS * HEAD_PAD + MLA_MIX), _const2),
            row(HEAD_PAD), row(HEAD_PAD), row(HEAD_PAD),
            tab_spec, tab_spec, tab_spec,
        ],
        out_specs=(q_spec, head_spec, vt_spec, pl.BlockSpec((1, TOK_BLOCK, MLA_MIX), _tok_map)),
        compiler_params=_params("arbitrary", "arbitrary"),
        name="mla_in",
    )(h, mod, norm_w, w_in, q_a_norm, w_uq, kv_a_norm, w_ukv, gq, gkn, gkr, cos_t, sin_lo, sin_hi)


def _attention_kernel(q_ref, k_ref, vt_ref, o_ref, s_ref, acc_ref, ot_ref):
    n_chunks = N_TOK // KV_CHUNK
    groups = KV_CHUNK // SUBLANES
    n_qb = SEQ // Q_BLOCK
    assert HEADS_PER_STEP == 2

    def q_rows(qb):
        return pl.ds(pl.multiple_of(qb * Q_BLOCK, Q_BLOCK), Q_BLOCK)

    def chunk(c):
        return slice(c * KV_CHUNK, (c + 1) * KV_CHUNK)

    def stages(score, value, mx_prev):
        mx = part = m_val = None
        if value is not None:
            m_val = jnp.max(mx_prev, axis=0, keepdims=True)
        for c in range(n_chunks):
            if score is not None:
                qb, hh = score
                sc = lax.dot_general(k_ref[0, hh, chunk(c), :], q_ref[0, hh, q_rows(qb), :],
                                     (((1,), (1,)), ((), ())), preferred_element_type=F32)
                s_ref[hh, chunk(c), :] = sc
                cm = jnp.max(sc.reshape(groups, SUBLANES, Q_BLOCK), axis=0)
                mx = cm if mx is None else jnp.maximum(mx, cm)
            if value is not None:
                qb, hh = value
                e = jnp.exp2(s_ref[hh, chunk(c), :] - m_val)
                cs = jnp.sum(e.reshape(groups, SUBLANES, Q_BLOCK), axis=0)
                part = cs if part is None else part + cs
                pv = _dot(vt_ref[0, hh, c], e.astype(BF16))
                if c == 0:
                    acc_ref[...] = pv
                else:
                    acc_ref[...] += pv
        if value is not None:
            qb, hh = value
            inv = 1.0 / jnp.sum(part, axis=0, keepdims=True)
            ot_ref[hh * V_HEAD:(hh + 1) * V_HEAD, :] = acc_ref[...] * inv
            if hh == HEADS_PER_STEP - 1:
                o_ref[0, q_rows(qb), :] = ot_ref[...].T
        return mx

    def block_pair(qb, mx_head0):
        mx_head1 = stages((qb, 1), (qb, 0), mx_head0)
        return stages((qb + 1, 0), (qb, 1), mx_head1)

    mx = stages((0, 0), None, None)
    mx = lax.fori_loop(0, n_qb - 1, block_pair, mx)
    mx = stages((n_qb - 1, 1), (n_qb - 1, 0), mx)
    stages(None, (n_qb - 1, 1), mx)


def _attention(q, k, vt):
    pairs = MLA_HEADS // HEADS_PER_STEP
    pair_map = lambda b, hp: (b, hp, 0, 0)
    return pl.pallas_call(
        _attention_kernel,
        out_shape=jax.ShapeDtypeStruct((BATCH, SEQ, MLA_MIX), F32),
        grid=(BATCH, pairs),
        in_specs=[
            pl.BlockSpec((1, HEADS_PER_STEP, SEQ, HEAD_PAD), pair_map),
            pl.BlockSpec((1, HEADS_PER_STEP, N_TOK, HEAD_PAD), pair_map),
            pl.BlockSpec((1, HEADS_PER_STEP, N_TOK // KV_CHUNK, V_HEAD, KV_CHUNK), lambda b, hp: (b, hp, 0, 0, 0)),
        ],
        out_specs=pl.BlockSpec((1, SEQ, HEADS_PER_STEP * V_HEAD), lambda b, hp: (b, 0, hp)),
        scratch_shapes=[
            pltpu.VMEM((2, N_TOK, Q_BLOCK), F32),
            pltpu.VMEM((V_HEAD, Q_BLOCK), F32),
            pltpu.VMEM((HEADS_PER_STEP * V_HEAD, Q_BLOCK), F32),
        ],
        compiler_params=_params("arbitrary", "arbitrary"),
        name="attention",
    )(q, k, vt)


def _mla_out_kernel(att_ref, gate_ref, h_ref, mod_ref, wo_ref, o_ref):
    mix = (att_ref[0] * gate_ref[0]).astype(BF16)
    o_ref[0] = h_ref[0] + mod_ref[0, 2:3, :] * _dot(mix, wo_ref[...])


def _mla_out(att, gate, h, mod, w_out):
    lat_tok = lambda b, t: (b, t + CTX_BLOCKS, 0)
    return pl.pallas_call(
        _mla_out_kernel,
        out_shape=jax.ShapeDtypeStruct((BATCH, SEQ, D_MODEL), F32),
        grid=(BATCH, SEQ // TOK_BLOCK),
        in_specs=[
            pl.BlockSpec((1, TOK_BLOCK, MLA_MIX), _tok_map),
            pl.BlockSpec((1, TOK_BLOCK, MLA_MIX), lat_tok),
            pl.BlockSpec((1, TOK_BLOCK, D_MODEL), lat_tok),
            pl.BlockSpec((1, 3, D_MODEL), lambda b, t: (b, 0, 0)),
            pl.BlockSpec((MLA_MIX, D_MODEL), _const2),
        ],
        out_specs=pl.BlockSpec((1, TOK_BLOCK, D_MODEL), _tok_map),
        compiler_params=_params("arbitrary", "arbitrary"),
        name="mla_out",
    )(att, gate, h, mod, w_out)


def _pad_heads(w, width, offset=0):
    k = w.shape[0]
    w = w.reshape(k, MLA_HEADS, width)
    return jnp.pad(w, ((0, 0), (0, 0), (offset, HEAD_PAD - width - offset))).reshape(k, MLA_HEADS * HEAD_PAD)


def _mla_weights(w_in, w_uq, w_ukv, q_norm, k_norm):
    cuts = [Q_LORA, Q_LORA + KV_LORA, Q_LORA + KV_LORA + QK_ROPE]
    w_kr = jnp.pad(w_in[:, cuts[1]:cuts[2]], ((0, 0), (QK_NOPE, LANES - QK_DIM)))
    w_in_p = jnp.concatenate([w_in[:, :cuts[1]], w_kr, w_in[:, cuts[2]:]], axis=1).astype(BF16)
    w_uq_p = _pad_heads(w_uq, QK_DIM).astype(BF16)
    w_kv = w_ukv.reshape(KV_LORA, MLA_HEADS, QK_NOPE + V_HEAD)
    w_k = _pad_heads(w_kv[:, :, :QK_NOPE].reshape(KV_LORA, -1), QK_NOPE)
    w_v = w_kv[:, :, QK_NOPE:].reshape(KV_LORA, MLA_MIX)
    w_ukv_p = jnp.concatenate([w_k, w_v], axis=1).astype(BF16)
    pad_row = lambda v, off: jnp.pad(v, (off, HEAD_PAD - v.shape[0] - off)).reshape(1, HEAD_PAD)
    gq = pad_row(q_norm, 0)
    gkn = pad_row(k_norm[:QK_NOPE], 0)
    gkr = pad_row(k_norm[QK_NOPE:], QK_NOPE)
    return w_in_p, w_uq_p, w_ukv_p, gq, gkn, gkr


def _rope_tables():
    rows = SEQ // GRID_W
    row = jnp.repeat(jnp.arange(rows, dtype=F32), GRID_W)
    col = jnp.tile(jnp.arange(GRID_W, dtype=F32), rows)
    n_freq = QK_ROPE // 4
    inv = jnp.power(ROPE_BASE, -jnp.arange(n_freq, dtype=F32) / n_freq)
    ang = jnp.concatenate([row[:, None] * inv, col[:, None] * inv], axis=-1)
    cos, sin = jnp.cos(ang), jnp.sin(ang)
    half = QK_ROPE // 2
    zeros = jnp.zeros_like(cos)
    pad_l = jnp.ones((SEQ, QK_NOPE), F32)
    tail = jnp.zeros((SEQ, HEAD_PAD - QK_DIM), F32)
    cos_t = jnp.concatenate([pad_l, cos, cos, tail], axis=-1)
    sin_lo = jnp.concatenate([0 * pad_l, -sin, zeros, tail], axis=-1)
    sin_hi = jnp.concatenate([0 * pad_l, zeros, sin, tail], axis=-1)
    ctx_cos = jnp.concatenate([jnp.ones((CTX_LEN, QK_DIM), F32), jnp.zeros((CTX_LEN, HEAD_PAD - QK_DIM), F32)], -1)
    ctx_zero = jnp.zeros((CTX_LEN, HEAD_PAD), F32)
    del half
    return (jnp.concatenate([ctx_cos, cos_t], 0), jnp.concatenate([ctx_zero, sin_lo], 0),
            jnp.concatenate([ctx_zero, sin_hi], 0))


def kernel(x, c, ctx, c_ctx, mod_w, mod_b, norm_w, e_w_in, e_conv_w, e_lam_re, e_lam_im, e_log_step, e_b_re, e_b_im, e_c_re, e_c_im, e_d, e_glu_w, e_glu_b, e_w_out, o_w_in, o_q_a_norm, o_w_uq, o_kv_a_norm, o_w_ukv, o_q_norm, o_k_norm, o_w_out):
    cond = jnp.zeros((COND_ROWS, D_MODEL), F32).at[:BATCH].set(c).at[CTX_ROW].set(c_ctx)
    mod = _modulation(cond, mod_w, mod_b)[:, :CTX_ROW + 1].reshape(2, CTX_ROW + 1, 3, D_MODEL)

    p, q, zs, u_tm = _even_in(ctx, x, mod[0], norm_w[0:1], e_w_in[0].astype(BF16))
    a_re, a_im, bbr, bbi = _s5_discretize(e_lam_re[0], e_lam_im[0], e_log_step[0], e_b_re[0], e_b_im[0])
    w_b, c_re, c_im = _s5_matrices(bbr, bbi, e_c_re[0], e_c_im[0])
    bcast = lambda a: jnp.broadcast_to(a[:, None, :], (2, BATCH, N_STATE))
    y = _s5_scan(u_tm.reshape(N_TOK * BATCH, SSM_WIDTH), w_b, c_re, c_im, bcast(a_re), bcast(a_im))
    h1 = _even_out(y.reshape(2, N_TOK, BATCH * SSM_WIDTH), u_tm, p, q, zs, ctx, x, mod[0],
                   e_d[0:1], e_glu_w[0].astype(BF16), e_glu_b[0:1], e_conv_w[0], e_w_out[0].astype(BF16))

    w_in_p, w_uq_p, w_ukv_p, gq, gkn, gkr = _mla_weights(o_w_in[0], o_w_uq[0], o_w_ukv[0], o_q_norm[0], o_k_norm[0])
    cos_t, sin_lo, sin_hi = _rope_tables()
    qh, kh, vh, gate = _mla_in(h1, mod[1], norm_w[1:2], w_in_p, o_q_a_norm[0:1], w_uq_p, o_kv_a_norm[0:1],
                               w_ukv_p, gq, gkn, gkr, cos_t, sin_lo, sin_hi)
    att = _attention(qh, kh, vh)
    return _mla_out(att, gate, h1, mod[1], o_w_out[0].astype(BF16))
```

```python
import functools
import math

import jax
import jax.numpy as jnp
from jax import lax
from jax.experimental import pallas as pl
from jax.experimental.pallas import tpu as pltpu

F32 = jnp.float32
BF16 = jnp.bfloat16

D_MODEL = 1024
BATCH = 8
SEQ = 4096
CTX_LEN = 256
GRID_W = 64
EPS = 1e-6
CONV_WIDTH = 512
SSM_WIDTH = 512
SSM_GROUP = 16
SSM_GROUPS = SSM_WIDTH // SSM_GROUP
SSM_STATE = 64
MLA_HEADS = 16
QK_NOPE = 64
QK_ROPE = 32
QK_DIM = QK_NOPE + QK_ROPE
V_HEAD = 64
Q_LORA = 384
KV_LORA = 256
MLA_MIX = MLA_HEADS * V_HEAD
ROPE_BASE = 10000.0

LANES = 128
SUBLANES = 8
VMEM_LIMIT_BYTES = 56 * 1024 * 1024

N_TOK = CTX_LEN + SEQ
TOK_BLOCK = 256
N_BLOCKS = N_TOK // TOK_BLOCK
CTX_BLOCKS = CTX_LEN // TOK_BLOCK
assert CTX_BLOCKS == 1 and N_TOK % TOK_BLOCK == 0
COND_ROWS = 2 * SUBLANES
CTX_ROW = BATCH
EVEN_IN = 4 * CONV_WIDTH + 2 * SSM_WIDTH

SCAN_STEPS = 64
SCAN_ROWS = SCAN_STEPS * BATCH
SCAN_CHUNKS = N_TOK // SCAN_STEPS
SCAN_CTX_CHUNKS = CTX_LEN // SCAN_STEPS
SLAB_GROUPS = LANES // SSM_GROUP
N_SLABS = SSM_GROUPS // SLAB_GROUPS
SLAB_STATE = SLAB_GROUPS * SSM_STATE
N_STATE = SSM_GROUPS * SSM_STATE

HEAD_PAD = LANES
SEG_TILE = 2 * LANES
Q_BLOCK = 512
HEADS_PER_STEP = 2
KV_CHUNK = 256
assert KV_CHUNK == TOK_BLOCK
ODD_IN_PAD = Q_LORA + KV_LORA + LANES + MLA_MIX
SOFTMAX_SCALE = QK_DIM ** -0.5 * math.log2(math.e)


def _params(*sem):
    return pltpu.CompilerParams(dimension_semantics=sem, vmem_limit_bytes=VMEM_LIMIT_BYTES)


def _dot(a, b):
    return jnp.dot(a, b, preferred_element_type=F32)


def _split_bf16(a):
    hi = a.astype(BF16)
    lo = (a - hi.astype(F32)).astype(BF16)
    return hi, lo


def _dot3(a, b):
    ah, al = _split_bf16(a)
    bh, bl = _split_bf16(b)
    return _dot(ah, bh) + (_dot(ah, bl) + _dot(al, bh))


def _rms(x, w):
    return x * lax.rsqrt(jnp.mean(x * x, axis=-1, keepdims=True) + EPS) * w


def _modulated_norm(h, norm_w, mod):
    return _rms(h, norm_w) * (1.0 + mod[1:2, :]) + mod[0:1, :]


def _mod_kernel(cond_ref, w_ref, b_ref, o_ref):
    cond = cond_ref[...]
    o_ref[0] = _dot3(cond * jax.nn.sigmoid(cond), w_ref[0]) + b_ref[0]


def _modulation(cond, mod_w, mod_b):
    depth = mod_w.shape[0]
    n_col = 3 * D_MODEL // D_MODEL
    return pl.pallas_call(
        _mod_kernel,
        out_shape=jax.ShapeDtypeStruct((depth, COND_ROWS, 3 * D_MODEL), F32),
        grid=(depth, n_col),
        in_specs=[
            pl.BlockSpec((COND_ROWS, D_MODEL), lambda i, j: (0, 0)),
            pl.BlockSpec((1, D_MODEL, D_MODEL), lambda i, j: (i, 0, j)),
            pl.BlockSpec((1, 1, D_MODEL), lambda i, j: (i, 0, j)),
        ],
        out_specs=pl.BlockSpec((1, COND_ROWS, D_MODEL), lambda i, j: (i, 0, j)),
        compiler_params=_params("arbitrary", "arbitrary"),
        name="modulation",
    )(cond, mod_w, mod_b.reshape(depth, 1, 3 * D_MODEL))


def _tok_map(b, t):
    return (b, t, 0)


def _mod_map(b, t):
    return (jnp.where(t < CTX_BLOCKS, CTX_ROW, b), 0, 0)


def _const2(b, t):
    return (0, 0)


TIME_BLOCK = TOK_BLOCK // BATCH
TIME_BLOCKS = N_TOK // TIME_BLOCK
CTX_TIME_BLOCKS = CTX_LEN // TIME_BLOCK


def _time_major_perm():
    r = jnp.arange(TOK_BLOCK)
    src = (r % BATCH) * TIME_BLOCK + r // BATCH
    return (src[:, None] == jnp.arange(TOK_BLOCK)[None, :]).astype(BF16)


def _ctx_time_map(i):
    return (0, jnp.minimum(i, CTX_TIME_BLOCKS - 1), 0)


def _lat_time_map(i):
    return (0, jnp.maximum(i - CTX_TIME_BLOCKS, 0), 0)


def _tok_time_map(i):
    return (0, i, 0)


def _mod_time_map(i):
    return (jnp.where(i < CTX_TIME_BLOCKS, 1, 0), 0, 0, 0)


def _const1(i):
    return (0, 0)


def _pick_hidden_time(ctx_ref, x_ref):
    return jnp.where(pl.program_id(0) < CTX_TIME_BLOCKS, ctx_ref[...], x_ref[...])


def _even_in_kernel(ctx_ref, x_ref, mod_ref, nw_ref, w_ref, perm_ref, p_ref, q_ref, zs_ref, u_ref, utm_ref):
    h = _pick_hidden_time(ctx_ref, x_ref)
    mod = mod_ref[0]
    a = _rms(h, nw_ref[...]) * (1.0 + mod[:, 1:2, :]) + mod[:, 0:1, :]
    r = _dot(a.reshape(TOK_BLOCK, D_MODEL).astype(BF16), w_ref[...])
    cw = CONV_WIDTH
    xa, ba, ca, za = (r[:, i * cw:(i + 1) * cw] for i in range(4))
    us = r[:, 4 * cw:4 * cw + SSM_WIDTH]
    zs = r[:, 4 * cw + SSM_WIDTH:]
    blk = (BATCH, TIME_BLOCK, CONV_WIDTH)
    p_ref[...] = (ca * xa).reshape(blk)
    q_ref[...] = (ba * (za * jax.nn.sigmoid(za))).reshape(blk)
    zs_ref[...] = (zs * jax.nn.sigmoid(zs)).reshape(blk)
    u_ref[...] = us.reshape(blk)
    utm_ref[...] = _dot(perm_ref[...], us.astype(BF16)).astype(BF16)


def _even_in(ctx, x, mod, norm_w, w_in, perm):
    tok = jax.ShapeDtypeStruct((BATCH, N_TOK, CONV_WIDTH), F32)
    tok_spec = pl.BlockSpec((BATCH, TIME_BLOCK, CONV_WIDTH), _tok_time_map)
    return pl.pallas_call(
        _even_in_kernel,
        out_shape=(tok, tok, tok, tok, jax.ShapeDtypeStruct((N_TOK * BATCH, SSM_WIDTH), BF16)),
        grid=(TIME_BLOCKS,),
        in_specs=[
            pl.BlockSpec((BATCH, TIME_BLOCK, D_MODEL), _ctx_time_map),
            pl.BlockSpec((BATCH, TIME_BLOCK, D_MODEL), _lat_time_map),
            pl.BlockSpec((1, BATCH, 3, D_MODEL), _mod_time_map),
            pl.BlockSpec((1, D_MODEL), _const1),
            pl.BlockSpec((D_MODEL, EVEN_IN), _const1),
            pl.BlockSpec((TOK_BLOCK, TOK_BLOCK), _const1),
        ],
        out_specs=(tok_spec, tok_spec, tok_spec, tok_spec,
                   pl.BlockSpec((TOK_BLOCK, SSM_WIDTH), lambda i: (i, 0))),
        compiler_params=_params("arbitrary"),
        name="even_in",
    )(ctx, x, mod, norm_w, w_in, perm)


def _s5_disc_kernel(lr_ref, li_ref, ls_ref, br_ref, bi_ref, ar_ref, ai_ref, bbr_ref, bbi_ref):
    lr = lr_ref[...]
    li = li_ref[...]
    dt = jnp.exp(ls_ref[...])
    mag = jnp.exp(lr * dt)
    ar = mag * jnp.cos(li * dt)
    ai = mag * jnp.sin(li * dt)
    nr = ar - 1.0
    den = lr * lr + li * li
    fr = (nr * lr + ai * li) / den
    fi = (ai * lr - nr * li) / den
    ar_ref[...] = ar
    ai_ref[...] = ai
    br = br_ref[...]
    bi = bi_ref[...]
    bbr_ref[...] = fr[:, None, :] * br - fi[:, None, :] * bi
    bbi_ref[...] = fr[:, None, :] * bi + fi[:, None, :] * br


def _s5_discretize(lam_re, lam_im, log_step, b_re, b_im):
    flat = lambda a: a.reshape(2, N_STATE)
    chan = lambda a: jnp.transpose(a, (0, 3, 1, 2)).reshape(2, SSM_GROUP, N_STATE)
    ls = jnp.repeat(log_step, SSM_STATE, axis=-1)
    vec = jax.ShapeDtypeStruct((2, N_STATE), F32)
    mat = jax.ShapeDtypeStruct((2, SSM_GROUP, N_STATE), F32)
    return pl.pallas_call(_s5_disc_kernel, out_shape=(vec, vec, mat, mat), name="s5_discretize")(
        flat(lam_re), flat(lam_im), ls, chan(b_re), chan(b_im))


def _block_diag_groups(a):
    d, j, g, r, c = a.shape
    eye = jnp.eye(g, dtype=bool)[None, None, :, None, :, None]
    full = jnp.where(eye, a[:, :, :, :, None, :], jnp.zeros((), a.dtype))
    return full.reshape(d, j, g * r, g * c)


def _s5_matrices(bbr, bbi, c_re, c_im):
    def in_map(bb):
        a = bb.reshape(2, SSM_GROUP, N_SLABS, SLAB_GROUPS, SSM_STATE)
        return _block_diag_groups(jnp.transpose(a, (0, 2, 3, 1, 4)))
    w_b = jnp.concatenate([in_map(bbr), in_map(bbi)], axis=-1).astype(BF16)

    def out_map(cc):
        a = cc.reshape(2, N_SLABS, SLAB_GROUPS, SSM_GROUP, SSM_STATE)
        return _block_diag_groups(jnp.transpose(a, (0, 1, 2, 4, 3))).astype(BF16)
    return w_b, out_map(c_re), out_map(c_im)


def _scan_chunk(d, i):
    back = jnp.where(i < SCAN_CTX_CHUNKS, SCAN_CTX_CHUNKS - 1 - i,
                     SCAN_CHUNKS - 1 - (i - SCAN_CTX_CHUNKS))
    return jnp.where(d == 0, i, back)


def _s5_scan_kernel(u_ref, wb_ref, cre_ref, cim_ref, ar_ref, ai_ref, y_ref, bu_ref, st_ref):
    d = pl.program_id(0)

    @pl.when(pl.program_id(1) == 0)
    def _():
        st_ref[...] = jnp.zeros_like(st_ref)

    ub = u_ref[...].astype(BF16)
    width = 2 * SLAB_STATE
    for j in range(N_SLABS):
        bu_ref[:, j * width:(j + 1) * width] = _dot(ub[:, j * LANES:(j + 1) * LANES], wb_ref[0, j])

    n_piece = 2 * N_SLABS

    def step(k, state):
        t = jnp.where(d == 0, k, SCAN_STEPS - 1 - k)
        rows = pl.ds(pl.multiple_of(t * BATCH, BATCH), BATCH)
        new = []
        for j in range(N_SLABS):
            s_re, s_im = state[2 * j], state[2 * j + 1]
            a_re = ar_ref[0, :, j * SLAB_STATE:(j + 1) * SLAB_STATE]
            a_im = ai_ref[0, :, j * SLAB_STATE:(j + 1) * SLAB_STATE]
            re_cols = slice(j * width, j * width + SLAB_STATE)
            im_cols = slice(j * width + SLAB_STATE, (j + 1) * width)
            n_re = a_re * s_re - a_im * s_im + bu_ref[rows, re_cols]
            n_im = a_re * s_im + a_im * s_re + bu_ref[rows, im_cols]
            bu_ref[rows, re_cols] = n_re
            bu_ref[rows, im_cols] = n_im
            new += [n_re, n_im]
        return tuple(new)

    init = tuple(st_ref[:, p * SLAB_STATE:(p + 1) * SLAB_STATE] for p in range(n_piece))
    final = lax.fori_loop(0, SCAN_STEPS, step, init, unroll=2)
    for p in range(n_piece):
        st_ref[:, p * SLAB_STATE:(p + 1) * SLAB_STATE] = final[p]

    for j in range(N_SLABS):
        s_re = bu_ref[:, j * width:j * width + SLAB_STATE].astype(BF16)
        s_im = bu_ref[:, j * width + SLAB_STATE:(j + 1) * width].astype(BF16)
        y_ref[0, :, j * LANES:(j + 1) * LANES] = _dot(s_re, cre_ref[0, j]) - _dot(s_im, cim_ref[0, j])


def _s5_scan(u_tm, w_b, c_re, c_im, a_re, a_im):
    dir_map = lambda d, i: (d, 0, 0, 0)
    vec_map = lambda d, i: (d, 0, 0)
    return pl.pallas_call(
        _s5_scan_kernel,
        out_shape=jax.ShapeDtypeStruct((2, N_TOK * BATCH, SSM_WIDTH), F32),
        grid=(2, SCAN_CHUNKS),
        in_specs=[
            pl.BlockSpec((SCAN_ROWS, SSM_WIDTH), lambda d, i: (_scan_chunk(d, i), 0)),
            pl.BlockSpec((1, N_SLABS, LANES, 2 * SLAB_STATE), dir_map),
            pl.BlockSpec((1, N_SLABS, SLAB_STATE, LANES), dir_map),
            pl.BlockSpec((1, N_SLABS, SLAB_STATE, LANES), dir_map),
            pl.BlockSpec((1, BATCH, N_STATE), vec_map),
            pl.BlockSpec((1, BATCH, N_STATE), vec_map),
        ],
        out_specs=pl.BlockSpec((1, SCAN_ROWS, SSM_WIDTH), lambda d, i: (d, _scan_chunk(d, i), 0)),
        scratch_shapes=[
            pltpu.VMEM((SCAN_ROWS, 2 * N_STATE), F32),
            pltpu.VMEM((BATCH, 2 * N_STATE), F32),
        ],
        compiler_params=_params("arbitrary", "arbitrary"),
        name="s5_scan",
    )(u_tm, w_b, c_re, c_im, a_re, a_im)


HALO_BLOCKS = TIME_BLOCK // SUBLANES


def _even_out_kernel(y_ref, u_ref, p_ref, pprev_ref, pnext_ref, q_ref, zs_ref, ctx_ref, x_ref, mod_ref,
                     d_ref, gw_ref, gb_ref, cw_ref, wo_ref, perm_ref, o_ref):
    i = pl.program_id(0)
    flat = lambda ref: ref[...].reshape(TOK_BLOCK, ref.shape[-1])
    y_hi, y_lo = _split_bf16(y_ref[0] + y_ref[1])
    y = _dot(perm_ref[...], y_hi) + _dot(perm_ref[...], y_lo) + d_ref[...] * flat(u_ref)
    g = jax.nn.gelu(y)
    z = _dot(g.astype(BF16), gw_ref[...]) + gb_ref[...]
    ssm = g * jax.nn.sigmoid(z) * flat(zs_ref)

    p = flat(p_ref)
    t_in_block = lax.broadcasted_iota(jnp.int32, p.shape, 0) % TIME_BLOCK
    prev_ok = jnp.logical_and(i != 0, i != CTX_TIME_BLOCKS).astype(F32)
    next_ok = jnp.logical_and(i != CTX_TIME_BLOCKS - 1, i != TIME_BLOCKS - 1).astype(F32)
    halo_shape = (BATCH, TIME_BLOCK, CONV_WIDTH)
    halo_prev = jnp.broadcast_to(pprev_ref[:, SUBLANES - 1:SUBLANES, :] * prev_ok, halo_shape)
    halo_next = jnp.broadcast_to(pnext_ref[:, 0:1, :] * next_ok, halo_shape)
    p_prev = jnp.where(t_in_block == 0, halo_prev.reshape(p.shape), pltpu.roll(p, 1, 0))
    p_next = jnp.where(t_in_block == TIME_BLOCK - 1, halo_next.reshape(p.shape), pltpu.roll(p, TOK_BLOCK - 1, 0))
    cw = cw_ref[...]
    conv = flat(q_ref) * (p_prev * cw[0:1, :] + p * cw[1:2, :] + p_next * cw[2:3, :])

    out = _dot(conv.astype(BF16), wo_ref[0:CONV_WIDTH, :]) + _dot(ssm.astype(BF16), wo_ref[CONV_WIDTH:, :])
    gate = mod_ref[0][:, 2:3, :]
    o_ref[...] = _pick_hidden_time(ctx_ref, x_ref) + gate * out.reshape(BATCH, TIME_BLOCK, D_MODEL)


def _even_out(y_tm, u, p, q, zs, ctx, x, mod, d_skip, glu_w, glu_b, conv_w, w_out, perm_t):
    halo_prev = lambda i: (0, jnp.maximum(i * HALO_BLOCKS - 1, 0), 0)
    halo_next = lambda i: (0, jnp.minimum((i + 1) * HALO_BLOCKS, N_TOK // SUBLANES - 1), 0)
    tok_spec = pl.BlockSpec((BATCH, TIME_BLOCK, CONV_WIDTH), _tok_time_map)
    halo = lambda m: pl.BlockSpec((BATCH, SUBLANES, CONV_WIDTH), m)
    return pl.pallas_call(
        _even_out_kernel,
        out_shape=jax.ShapeDtypeStruct((BATCH, N_TOK, D_MODEL), F32),
        grid=(TIME_BLOCKS,),
        in_specs=[
            pl.BlockSpec((2, TOK_BLOCK, SSM_WIDTH), lambda i: (0, i, 0)),
            tok_spec, tok_spec, halo(halo_prev), halo(halo_next), tok_spec, tok_spec,
            pl.BlockSpec((BATCH, TIME_BLOCK, D_MODEL), _ctx_time_map),
            pl.BlockSpec((BATCH, TIME_BLOCK, D_MODEL), _lat_time_map),
            pl.BlockSpec((1, BATCH, 3, D_MODEL), _mod_time_map),
            pl.BlockSpec((1, SSM_WIDTH), _const1),
            pl.BlockSpec((SSM_WIDTH, SSM_WIDTH), _const1),
            pl.BlockSpec((1, SSM_WIDTH), _const1),
            pl.BlockSpec((3, CONV_WIDTH), _const1),
            pl.BlockSpec((CONV_WIDTH + SSM_WIDTH, D_MODEL), _const1),
            pl.BlockSpec((TOK_BLOCK, TOK_BLOCK), _const1),
        ],
        out_specs=pl.BlockSpec((BATCH, TIME_BLOCK, D_MODEL), _tok_time_map),
        compiler_params=_params("arbitrary"),
        name="even_out",
    )(y_tm, u, p, p, p, q, zs, ctx, x, mod, d_skip, glu_w, glu_b, conv_w, w_out, perm_t)


def _rope(v, cos_t, sin_lo, sin_hi):
    half = QK_ROPE // 2
    return v * cos_t + pltpu.roll(v, LANES - half, 1) * sin_lo + pltpu.roll(v, half, 1) * sin_hi


def _mla_in_kernel(h_ref, mod_ref, nw_ref, win_ref, qan_ref, wuq_ref, kvan_ref, wukv_ref,
                   gq_ref, gkn_ref, gkr_ref, segq_ref, segk_ref, lenq_ref, cos_ref, slo_ref, shi_ref,
                   q_ref, k_ref, vt_ref, gate_ref):
    a = _modulated_norm(h_ref[0], nw_ref[...], mod_ref[0])
    r = _dot(a.astype(BF16), win_ref[...])
    cq = r[:, :Q_LORA]
    ckv = r[:, Q_LORA:Q_LORA + KV_LORA]
    kr = r[:, Q_LORA + KV_LORA:Q_LORA + KV_LORA + LANES]
    gate = r[:, Q_LORA + KV_LORA + LANES:]
    gate_ref[0] = gate * jax.nn.sigmoid(gate)

    qf = _dot(_rms(cq, qan_ref[...]).astype(BF16), wuq_ref[...])
    kvf = _dot(_rms(ckv, kvan_ref[...]).astype(BF16), wukv_ref[...])

    cos_t, sin_lo, sin_hi = cos_ref[...], slo_ref[...], shi_ref[...]
    kr_ms = jnp.sum(kr * kr, axis=-1, keepdims=True) * (1.0 / QK_ROPE)
    k_rope = _rope(kr * lax.rsqrt(kr_ms + EPS) * gkr_ref[...], cos_t, sin_lo, sin_hi)

    for j in range(MLA_HEADS * HEAD_PAD // SEG_TILE):
        cols = slice(j * SEG_TILE, (j + 1) * SEG_TILE)
        qj = qf[:, cols]
        q_ms = _dot((qj * qj).astype(BF16), segq_ref[...]) * lenq_ref[...]
        qn = qj * lax.rsqrt(q_ms + EPS) * gq_ref[...]
        kj = kvf[:, cols]
        k_ms = _dot((kj * kj).astype(BF16), segk_ref[...]) * (1.0 / QK_NOPE)
        kn = kj * lax.rsqrt(k_ms + EPS) * gkn_ref[...]
        for i in range(SEG_TILE // HEAD_PAD):
            h = j * (SEG_TILE // HEAD_PAD) + i
            lanes = slice(i * HEAD_PAD, (i + 1) * HEAD_PAD)
            q_ref[0, h, 0] = (_rope(qn[:, lanes], cos_t, sin_lo, sin_hi) * SOFTMAX_SCALE).T.astype(BF16)
            k_ref[0, h] = (kn[:, lanes] + k_rope).astype(BF16)
    vt = kvf[:, MLA_HEADS * HEAD_PAD:].T.astype(BF16)
    vt_ref[0, :, 0] = vt.reshape(MLA_HEADS, V_HEAD, TOK_BLOCK)


def _mla_in(h, mod, norm_w, w_in, q_a_norm, w_uq, kv_a_norm, w_ukv, gq, gkn, gkr, seg_q, seg_k, len_q,
            cos_t, sin_lo, sin_hi):
    head = jax.ShapeDtypeStruct((BATCH, MLA_HEADS, N_TOK, HEAD_PAD), BF16)
    head_spec = pl.BlockSpec((1, MLA_HEADS, TOK_BLOCK, HEAD_PAD), lambda b, t: (b, 0, t, 0))
    per_qb = Q_BLOCK // TOK_BLOCK
    q_lat = jax.ShapeDtypeStruct((BATCH, MLA_HEADS, SEQ // Q_BLOCK, HEAD_PAD, Q_BLOCK), BF16)

    def q_map(b, t):
        lat = jnp.maximum(t - CTX_BLOCKS, 0)
        return (b, 0, lat // per_qb, 0, lat % per_qb)
    q_spec = pl.BlockSpec((1, MLA_HEADS, 1, HEAD_PAD, TOK_BLOCK), q_map)
    vt = jax.ShapeDtypeStruct((BATCH, MLA_HEADS, N_TOK // KV_CHUNK, V_HEAD, KV_CHUNK), BF16)
    vt_spec = pl.BlockSpec((1, MLA_HEADS, 1, V_HEAD, KV_CHUNK), lambda b, t: (b, 0, t, 0, 0))
    tab_spec = pl.BlockSpec((TOK_BLOCK, HEAD_PAD), lambda b, t: (t, 0))
    row = lambda n: pl.BlockSpec((1, n), _const2)
    return pl.pallas_call(
        _mla_in_kernel,
        out_shape=(q_lat, head, vt, jax.ShapeDtypeStruct((BATCH, N_TOK, MLA_MIX), F32)),
        grid=(BATCH, N_BLOCKS),
        in_specs=[
            pl.BlockSpec((1, TOK_BLOCK, D_MODEL), _tok_map),
            pl.BlockSpec((1, 3, D_MODEL), _mod_map),
            row(D_MODEL),
            pl.BlockSpec((D_MODEL, ODD_IN_PAD), _const2),
            row(Q_LORA),
            pl.BlockSpec((Q_LORA, MLA_HEADS * HEAD_PAD), _const2),
            row(KV_LORA),
            pl.BlockSpec((KV_LORA, MLA_HEADS * HEAD_PAD + MLA_MIX), _const2),
            row(SEG_TILE), row(SEG_TILE), row(HEAD_PAD),
            pl.BlockSpec((SEG_TILE, SEG_TILE), _const2), pl.BlockSpec((SEG_TILE, SEG_TILE), _const2),
            row(SEG_TILE),
            tab_spec, tab_spec, tab_spec,
        ],
        out_specs=(q_spec, head_spec, vt_spec, pl.BlockSpec((1, TOK_BLOCK, MLA_MIX), _tok_map)),
        compiler_params=_params("arbitrary", "arbitrary"),
        name="mla_in",
    )(h, mod, norm_w, w_in, q_a_norm, w_uq, kv_a_norm, w_ukv, gq, gkn, gkr, seg_q, seg_k, len_q,
      cos_t, sin_lo, sin_hi)


def _attention_kernel(q_ref, k_ref, vt_ref, o_ref, s_ref, ot_ref):
    n_chunks = N_TOK // KV_CHUNK
    groups = KV_CHUNK // SUBLANES
    n_qb = SEQ // Q_BLOCK
    assert HEADS_PER_STEP == 2

    def q_rows(qb):
        return pl.ds(pl.multiple_of(qb * Q_BLOCK, Q_BLOCK), Q_BLOCK)

    def chunk(c):
        return slice(c * KV_CHUNK, (c + 1) * KV_CHUNK)

    def stages(score, value, mx_prev):
        mx = part = m_val = None
        if value is not None:
            m_val = jnp.max(mx_prev, axis=0, keepdims=True)
        for c in range(n_chunks):
            if score is not None:
                qb, hh = score
                sc = _dot(k_ref[0, hh, chunk(c), :], q_ref[0, hh, qb])
                s_ref[hh, chunk(c), :] = sc
                cm = jnp.max(sc.reshape(groups, SUBLANES, Q_BLOCK), axis=0)
                mx = cm if mx is None else jnp.maximum(mx, cm)
            if value is not None:
                qb, hh = value
                e = jnp.exp2(s_ref[hh, chunk(c), :] - m_val)
                cs = jnp.sum(e.reshape(groups, SUBLANES, Q_BLOCK), axis=0)
                part = cs if part is None else part + cs
                pv = _dot(vt_ref[0, hh, c], e.astype(BF16))
                acc = pv if c == 0 else acc + pv
        if value is not None:
            qb, hh = value
            inv = 1.0 / jnp.sum(part, axis=0, keepdims=True)
            ot_ref[hh * V_HEAD:(hh + 1) * V_HEAD, :] = acc * inv
            if hh == HEADS_PER_STEP - 1:
                o_ref[0, q_rows(qb), :] = ot_ref[...].T
        return mx

    def block_pair(qb, mx_head0):
        mx_head1 = stages((qb, 1), (qb, 0), mx_head0)
        return stages((qb + 1, 0), (qb, 1), mx_head1)

    mx = stages((0, 0), None, None)
    mx = lax.fori_loop(0, n_qb - 1, block_pair, mx)
    mx = stages((n_qb - 1, 1), (n_qb - 1, 0), mx)
    stages(None, (n_qb - 1, 1), mx)


def _attention(q, k, vt):
    pairs = MLA_HEADS // HEADS_PER_STEP
    pair_map = lambda b, hp: (b, hp, 0, 0)
    return pl.pallas_call(
        _attention_kernel,
        out_shape=jax.ShapeDtypeStruct((BATCH, SEQ, MLA_MIX), F32),
        grid=(BATCH, pairs),
        in_specs=[
            pl.BlockSpec((1, HEADS_PER_STEP, SEQ // Q_BLOCK, HEAD_PAD, Q_BLOCK), lambda b, hp: (b, hp, 0, 0, 0)),
            pl.BlockSpec((1, HEADS_PER_STEP, N_TOK, HEAD_PAD), pair_map),
            pl.BlockSpec((1, HEADS_PER_STEP, N_TOK // KV_CHUNK, V_HEAD, KV_CHUNK), lambda b, hp: (b, hp, 0, 0, 0)),
        ],
        out_specs=pl.BlockSpec((1, SEQ, HEADS_PER_STEP * V_HEAD), lambda b, hp: (b, 0, hp)),
        scratch_shapes=[
            pltpu.VMEM((2, N_TOK, Q_BLOCK), F32),
            pltpu.VMEM((HEADS_PER_STEP * V_HEAD, Q_BLOCK), F32),
        ],
        compiler_params=_params("arbitrary", "arbitrary"),
        name="attention",
    )(q, k, vt)


def _mla_out_kernel(att_ref, gate_ref, h_ref, mod_ref, wo_ref, o_ref):
    mix = (att_ref[0] * gate_ref[0]).astype(BF16)
    o_ref[0] = h_ref[0] + mod_ref[0, 2:3, :] * _dot(mix, wo_ref[...])


def _mla_out(att, gate, h, mod, w_out):
    lat_tok = lambda b, t: (b, t + CTX_BLOCKS, 0)
    return pl.pallas_call(
        _mla_out_kernel,
        out_shape=jax.ShapeDtypeStruct((BATCH, SEQ, D_MODEL), F32),
        grid=(BATCH, SEQ // TOK_BLOCK),
        in_specs=[
            pl.BlockSpec((1, TOK_BLOCK, MLA_MIX), _tok_map),
            pl.BlockSpec((1, TOK_BLOCK, MLA_MIX), lat_tok),
            pl.BlockSpec((1, TOK_BLOCK, D_MODEL), lat_tok),
            pl.BlockSpec((1, 3, D_MODEL), lambda b, t: (b, 0, 0)),
            pl.BlockSpec((MLA_MIX, D_MODEL), _const2),
        ],
        out_specs=pl.BlockSpec((1, TOK_BLOCK, D_MODEL), _tok_map),
        compiler_params=_params("arbitrary", "arbitrary"),
        name="mla_out",
    )(att, gate, h, mod, w_out)


def _pad_heads(w, width, offset=0):
    k = w.shape[0]
    w = w.reshape(k, MLA_HEADS, width)
    return jnp.pad(w, ((0, 0), (0, 0), (offset, HEAD_PAD - width - offset))).reshape(k, MLA_HEADS * HEAD_PAD)


def _mla_weights(w_in, w_uq, w_ukv, q_norm, k_norm):
    cuts = [Q_LORA, Q_LORA + KV_LORA, Q_LORA + KV_LORA + QK_ROPE]
    w_kr = jnp.pad(w_in[:, cuts[1]:cuts[2]], ((0, 0), (QK_NOPE, LANES - QK_DIM)))
    w_in_p = jnp.concatenate([w_in[:, :cuts[1]], w_kr, w_in[:, cuts[2]:]], axis=1).astype(BF16)
    w_uq_p = _pad_heads(w_uq, QK_DIM).astype(BF16)
    w_kv = w_ukv.reshape(KV_LORA, MLA_HEADS, QK_NOPE + V_HEAD)
    w_k = _pad_heads(w_kv[:, :, :QK_NOPE].reshape(KV_LORA, -1), QK_NOPE)
    w_v = w_kv[:, :, QK_NOPE:].reshape(KV_LORA, MLA_MIX)
    w_ukv_p = jnp.concatenate([w_k, w_v], axis=1).astype(BF16)
    pad_row = lambda v, off: jnp.pad(v, (off, HEAD_PAD - v.shape[0] - off)).reshape(1, HEAD_PAD)
    per_tile = SEG_TILE // HEAD_PAD
    gq = jnp.tile(pad_row(q_norm, 0), (1, per_tile))
    gkn = jnp.tile(pad_row(k_norm[:QK_NOPE], 0), (1, per_tile))
    gkr = pad_row(k_norm[QK_NOPE:], QK_NOPE)
    return w_in_p, w_uq_p, w_ukv_p, gq, gkn, gkr


def _segment_matrices():
    lane = jnp.arange(SEG_TILE)
    head = lane // HEAD_PAD
    in_head = lane % HEAD_PAD
    seg = jnp.where(in_head < QK_NOPE, 0, jnp.where(in_head < QK_DIM, 1, 2))
    same_head = head[:, None] == head[None, :]
    seg_q = same_head & (seg[:, None] == seg[None, :]) & (seg[:, None] < 2)
    seg_k = same_head & (seg[:, None] == 0) & (seg[None, :] == 0)
    len_q = jnp.where(seg == 0, 1.0 / QK_NOPE, 1.0 / QK_ROPE).astype(F32).reshape(1, SEG_TILE)
    return seg_q.astype(BF16), seg_k.astype(BF16), len_q


def _rope_tables():
    rows = SEQ // GRID_W
    row = jnp.repeat(jnp.arange(rows, dtype=F32), GRID_W)
    col = jnp.tile(jnp.arange(GRID_W, dtype=F32), rows)
    n_freq = QK_ROPE // 4
    inv = jnp.power(ROPE_BASE, -jnp.arange(n_freq, dtype=F32) / n_freq)
    ang = jnp.concatenate([row[:, None] * inv, col[:, None] * inv], axis=-1)
    cos, sin = jnp.cos(ang), jnp.sin(ang)
    half = QK_ROPE // 2
    zeros = jnp.zeros_like(cos)
    pad_l = jnp.ones((SEQ, QK_NOPE), F32)
    tail = jnp.zeros((SEQ, HEAD_PAD - QK_DIM), F32)
    cos_t = jnp.concatenate([pad_l, cos, cos, tail], axis=-1)
    sin_lo = jnp.concatenate([0 * pad_l, -sin, zeros, tail], axis=-1)
    sin_hi = jnp.concatenate([0 * pad_l, zeros, sin, tail], axis=-1)
    ctx_cos = jnp.concatenate([jnp.ones((CTX_LEN, QK_DIM), F32), jnp.zeros((CTX_LEN, HEAD_PAD - QK_DIM), F32)], -1)
    ctx_zero = jnp.zeros((CTX_LEN, HEAD_PAD), F32)
    del half
    return (jnp.concatenate([ctx_cos, cos_t], 0), jnp.concatenate([ctx_zero, sin_lo], 0),
            jnp.concatenate([ctx_zero, sin_hi], 0))


def kernel(x, c, ctx, c_ctx, mod_w, mod_b, norm_w, e_w_in, e_conv_w, e_lam_re, e_lam_im, e_log_step, e_b_re, e_b_im, e_c_re, e_c_im, e_d, e_glu_w, e_glu_b, e_w_out, o_w_in, o_q_a_norm, o_w_uq, o_kv_a_norm, o_w_ukv, o_q_norm, o_k_norm, o_w_out):
    cond = jnp.zeros((COND_ROWS, D_MODEL), F32).at[:BATCH].set(c).at[CTX_ROW].set(c_ctx)
    mod = _modulation(cond, mod_w, mod_b)[:, :CTX_ROW + 1].reshape(2, CTX_ROW + 1, 3, D_MODEL)

    mod_even = jnp.stack([mod[0, :BATCH], jnp.broadcast_to(mod[0, CTX_ROW], (BATCH, 3, D_MODEL))])
    perm = _time_major_perm()
    p, q, zs, u, u_tm = _even_in(ctx, x, mod_even, norm_w[0:1], e_w_in[0].astype(BF16), perm)
    a_re, a_im, bbr, bbi = _s5_discretize(e_lam_re[0], e_lam_im[0], e_log_step[0], e_b_re[0], e_b_im[0])
    w_b, c_re, c_im = _s5_matrices(bbr, bbi, e_c_re[0], e_c_im[0])
    bcast = lambda a: jnp.broadcast_to(a[:, None, :], (2, BATCH, N_STATE))
    y = _s5_scan(u_tm, w_b, c_re, c_im, bcast(a_re), bcast(a_im))
    h1 = _even_out(y, u, p, q, zs, ctx, x, mod_even, e_d[0:1], e_glu_w[0].astype(BF16), e_glu_b[0:1],
                   e_conv_w[0], e_w_out[0].astype(BF16), perm.T)

    w_in_p, w_uq_p, w_ukv_p, gq, gkn, gkr = _mla_weights(o_w_in[0], o_w_uq[0], o_w_ukv[0], o_q_norm[0], o_k_norm[0])
    cos_t, sin_lo, sin_hi = _rope_tables()
    seg_q, seg_k, len_q = _segment_matrices()
    qh, kh, vh, gate = _mla_in(h1, mod[1], norm_w[1:2], w_in_p, o_q_a_norm[0:1], w_uq_p, o_kv_a_norm[0:1],
                               w_ukv_p, gq, gkn, gkr, seg_q, seg_k, len_q, cos_t, sin_lo, sin_hi)
    att = _attention(qh, kh, vh)
    return _mla_out(att, gate, h1, mod[1], o_w_out[0].astype(BF16))
```

```python
import functools
import math

import jax
import jax.numpy as jnp
from jax import lax
from jax.experimental import pallas as pl
from jax.experimental.pallas import tpu as pltpu

F32 = jnp.float32
BF16 = jnp.bfloat16

D_MODEL = 1024
BATCH = 8
SEQ = 4096
CTX_LEN = 256
GRID_W = 64
EPS = 1e-6
CONV_WIDTH = 512
SSM_WIDTH = 512
SSM_GROUP = 16
SSM_GROUPS = SSM_WIDTH // SSM_GROUP
SSM_STATE = 64
MLA_HEADS = 16
QK_NOPE = 64
QK_ROPE = 32
QK_DIM = QK_NOPE + QK_ROPE
V_HEAD = 64
Q_LORA = 384
KV_LORA = 256
MLA_MIX = MLA_HEADS * V_HEAD
ROPE_BASE = 10000.0

LANES = 128
SUBLANES = 8
VMEM_LIMIT_BYTES = 56 * 1024 * 1024

N_TOK = CTX_LEN + SEQ
TOK_BLOCK = 256
N_BLOCKS = N_TOK // TOK_BLOCK
CTX_BLOCKS = CTX_LEN // TOK_BLOCK
assert CTX_BLOCKS == 1 and N_TOK % TOK_BLOCK == 0
COND_ROWS = 2 * SUBLANES
CTX_ROW = BATCH
EVEN_IN = 4 * CONV_WIDTH + 2 * SSM_WIDTH

SCAN_STEPS = 64
SCAN_ROWS = SCAN_STEPS * BATCH
SCAN_CHUNKS = N_TOK // SCAN_STEPS
SCAN_CTX_CHUNKS = CTX_LEN // SCAN_STEPS
SLAB_GROUPS = LANES // SSM_GROUP
N_SLABS = SSM_GROUPS // SLAB_GROUPS
SLAB_STATE = SLAB_GROUPS * SSM_STATE
N_STATE = SSM_GROUPS * SSM_STATE

HEAD_PAD = LANES
SEG_TILE = 2 * LANES
Q_BLOCK = 512
HEADS_PER_STEP = 4
KV_CHUNK = 256
V_ROWS = V_HEAD + 2 * SUBLANES
assert KV_CHUNK == TOK_BLOCK
ODD_IN_PAD = Q_LORA + KV_LORA + LANES + MLA_MIX
SOFTMAX_SCALE = QK_DIM ** -0.5 * math.log2(math.e)


def _params(*sem):
    return pltpu.CompilerParams(dimension_semantics=sem, vmem_limit_bytes=VMEM_LIMIT_BYTES)


def _dot(a, b):
    return jnp.dot(a, b, preferred_element_type=F32)


def _split_bf16(a):
    hi = a.astype(BF16)
    lo = (a - hi.astype(F32)).astype(BF16)
    return hi, lo


def _dot3(a, b):
    ah, al = _split_bf16(a)
    bh, bl = _split_bf16(b)
    return _dot(ah, bh) + (_dot(ah, bl) + _dot(al, bh))


def _rms(x, w):
    return x * lax.rsqrt(jnp.mean(x * x, axis=-1, keepdims=True) + EPS) * w


def _modulated_norm(h, norm_w, mod):
    return _rms(h, norm_w) * (1.0 + mod[1:2, :]) + mod[0:1, :]


def _mod_kernel(cond_ref, w_ref, b_ref, o_ref):
    cond = cond_ref[...]
    o_ref[0] = _dot3(cond * jax.nn.sigmoid(cond), w_ref[0]) + b_ref[0]


def _modulation(cond, mod_w, mod_b):
    depth = mod_w.shape[0]
    n_col = 3 * D_MODEL // D_MODEL
    return pl.pallas_call(
        _mod_kernel,
        out_shape=jax.ShapeDtypeStruct((depth, COND_ROWS, 3 * D_MODEL), F32),
        grid=(depth, n_col),
        in_specs=[
            pl.BlockSpec((COND_ROWS, D_MODEL), lambda i, j: (0, 0)),
            pl.BlockSpec((1, D_MODEL, D_MODEL), lambda i, j: (i, 0, j)),
            pl.BlockSpec((1, 1, D_MODEL), lambda i, j: (i, 0, j)),
        ],
        out_specs=pl.BlockSpec((1, COND_ROWS, D_MODEL), lambda i, j: (i, 0, j)),
        compiler_params=_params("arbitrary", "arbitrary"),
        name="modulation",
    )(cond, mod_w, mod_b.reshape(depth, 1, 3 * D_MODEL))


def _tok_map(b, t):
    return (b, t, 0)


def _mod_map(b, t):
    return (jnp.where(t < CTX_BLOCKS, CTX_ROW, b), 0, 0)


def _const2(b, t):
    return (0, 0)


TIME_BLOCK = TOK_BLOCK // BATCH
TIME_BLOCKS = N_TOK // TIME_BLOCK
CTX_TIME_BLOCKS = CTX_LEN // TIME_BLOCK


def _time_major_perm():
    r = jnp.arange(TOK_BLOCK)
    src = (r % BATCH) * TIME_BLOCK + r // BATCH
    return (src[:, None] == jnp.arange(TOK_BLOCK)[None, :]).astype(BF16)


def _ctx_time_map(i):
    return (0, jnp.minimum(i, CTX_TIME_BLOCKS - 1), 0)


def _lat_time_map(i):
    return (0, jnp.maximum(i - CTX_TIME_BLOCKS, 0), 0)


def _tok_time_map(i):
    return (0, i, 0)


def _mod_time_map(i):
    return (jnp.where(i < CTX_TIME_BLOCKS, 1, 0), 0, 0, 0)


def _const1(i):
    return (0, 0)


def _pick_hidden_time(ctx_ref, x_ref):
    return jnp.where(pl.program_id(0) < CTX_TIME_BLOCKS, ctx_ref[...], x_ref[...])


def _even_in_kernel(ctx_ref, x_ref, mod_ref, nw_ref, w_ref, perm_ref, p_ref, q_ref, zs_ref, u_ref, utm_ref):
    h = _pick_hidden_time(ctx_ref, x_ref)
    mod = mod_ref[0]
    a = _rms(h, nw_ref[...]) * (1.0 + mod[:, 1:2, :]) + mod[:, 0:1, :]
    r = _dot(a.reshape(TOK_BLOCK, D_MODEL).astype(BF16), w_ref[...])
    cw = CONV_WIDTH
    xa, ba, ca, za = (r[:, i * cw:(i + 1) * cw] for i in range(4))
    us = r[:, 4 * cw:4 * cw + SSM_WIDTH]
    zs = r[:, 4 * cw + SSM_WIDTH:]
    blk = (BATCH, TIME_BLOCK, CONV_WIDTH)
    p_ref[...] = (ca * xa).reshape(blk)
    q_ref[...] = (ba * (za * jax.nn.sigmoid(za))).reshape(blk)
    zs_ref[...] = (zs * jax.nn.sigmoid(zs)).reshape(blk)
    u_ref[...] = us.reshape(blk)
    utm_ref[...] = _dot(perm_ref[...], us.astype(BF16)).astype(BF16)


def _even_in(ctx, x, mod, norm_w, w_in, perm):
    tok = jax.ShapeDtypeStruct((BATCH, N_TOK, CONV_WIDTH), F32)
    tok_spec = pl.BlockSpec((BATCH, TIME_BLOCK, CONV_WIDTH), _tok_time_map)
    return pl.pallas_call(
        _even_in_kernel,
        out_shape=(tok, tok, tok, tok, jax.ShapeDtypeStruct((N_TOK * BATCH, SSM_WIDTH), BF16)),
        grid=(TIME_BLOCKS,),
        in_specs=[
            pl.BlockSpec((BATCH, TIME_BLOCK, D_MODEL), _ctx_time_map),
            pl.BlockSpec((BATCH, TIME_BLOCK, D_MODEL), _lat_time_map),
            pl.BlockSpec((1, BATCH, 3, D_MODEL), _mod_time_map),
            pl.BlockSpec((1, D_MODEL), _const1),
            pl.BlockSpec((D_MODEL, EVEN_IN), _const1),
            pl.BlockSpec((TOK_BLOCK, TOK_BLOCK), _const1),
        ],
        out_specs=(tok_spec, tok_spec, tok_spec, tok_spec,
                   pl.BlockSpec((TOK_BLOCK, SSM_WIDTH), lambda i: (i, 0))),
        compiler_params=_params("arbitrary"),
        name="even_in",
    )(ctx, x, mod, norm_w, w_in, perm)


def _s5_disc_kernel(lr_ref, li_ref, ls_ref, br_ref, bi_ref, ar_ref, ai_ref, bbr_ref, bbi_ref):
    lr = lr_ref[...]
    li = li_ref[...]
    dt = jnp.exp(ls_ref[...])
    mag = jnp.exp(lr * dt)
    ar = mag * jnp.cos(li * dt)
    ai = mag * jnp.sin(li * dt)
    nr = ar - 1.0
    den = lr * lr + li * li
    fr = (nr * lr + ai * li) / den
    fi = (ai * lr - nr * li) / den
    ar_ref[...] = ar
    ai_ref[...] = ai
    br = br_ref[...]
    bi = bi_ref[...]
    bbr_ref[...] = fr[:, None, :] * br - fi[:, None, :] * bi
    bbi_ref[...] = fr[:, None, :] * bi + fi[:, None, :] * br


def _s5_discretize(lam_re, lam_im, log_step, b_re, b_im):
    flat = lambda a: a.reshape(2, N_STATE)
    chan = lambda a: jnp.transpose(a, (0, 3, 1, 2)).reshape(2, SSM_GROUP, N_STATE)
    ls = jnp.repeat(log_step, SSM_STATE, axis=-1)
    vec = jax.ShapeDtypeStruct((2, N_STATE), F32)
    mat = jax.ShapeDtypeStruct((2, SSM_GROUP, N_STATE), F32)
    return pl.pallas_call(_s5_disc_kernel, out_shape=(vec, vec, mat, mat), name="s5_discretize")(
        flat(lam_re), flat(lam_im), ls, chan(b_re), chan(b_im))


def _block_diag_groups(a):
    d, j, g, r, c = a.shape
    eye = jnp.eye(g, dtype=bool)[None, None, :, None, :, None]
    full = jnp.where(eye, a[:, :, :, :, None, :], jnp.zeros((), a.dtype))
    return full.reshape(d, j, g * r, g * c)


def _s5_matrices(bbr, bbi, c_re, c_im):
    def in_map(bb):
        a = bb.reshape(2, SSM_GROUP, N_SLABS, SLAB_GROUPS, SSM_STATE)
        return _block_diag_groups(jnp.transpose(a, (0, 2, 3, 1, 4)))
    w_b = jnp.concatenate([in_map(bbr), in_map(bbi)], axis=-1).astype(BF16)

    def out_map(cc):
        a = cc.reshape(2, N_SLABS, SLAB_GROUPS, SSM_GROUP, SSM_STATE)
        return _block_diag_groups(jnp.transpose(a, (0, 1, 2, 4, 3))).astype(BF16)
    return w_b, out_map(c_re), out_map(c_im)


def _scan_chunk(d, i):
    back = jnp.where(i < SCAN_CTX_CHUNKS, SCAN_CTX_CHUNKS - 1 - i,
                     SCAN_CHUNKS - 1 - (i - SCAN_CTX_CHUNKS))
    return jnp.where(d == 0, i, back)


SCAN_BUFFERS = 3


def _s5_scan_kernel(ucur_ref, unext_ref, wb_ref, cre_ref, cim_ref, ar_ref, ai_ref, y_ref, *scratch):
    bufs, st_ref = scratch[:SCAN_BUFFERS], scratch[SCAN_BUFFERS]
    d = pl.program_id(0)
    i = pl.program_id(1)
    width = 2 * SLAB_STATE
    n_piece = 2 * N_SLABS

    def input_map(u_ref, buf):
        ub = u_ref[...]
        for j in range(N_SLABS):
            buf[:, j * width:(j + 1) * width] = _dot(ub[:, j * LANES:(j + 1) * LANES], wb_ref[0, j])

    def recurrence(buf):
        state = [st_ref[:, p * SLAB_STATE:(p + 1) * SLAB_STATE] for p in range(n_piece)]
        for k in range(SCAN_STEPS):
            t = jnp.where(d == 0, k, SCAN_STEPS - 1 - k)
            rows = pl.ds(pl.multiple_of(t * BATCH, BATCH), BATCH)
            for j in range(N_SLABS):
                s_re, s_im = state[2 * j], state[2 * j + 1]
                a_re = ar_ref[0, :, j * SLAB_STATE:(j + 1) * SLAB_STATE]
                a_im = ai_ref[0, :, j * SLAB_STATE:(j + 1) * SLAB_STATE]
                re_cols = slice(j * width, j * width + SLAB_STATE)
                im_cols = slice(j * width + SLAB_STATE, (j + 1) * width)
                n_re = a_re * s_re - a_im * s_im + buf[rows, re_cols]
                n_im = a_re * s_im + a_im * s_re + buf[rows, im_cols]
                buf[rows, re_cols] = n_re
                buf[rows, im_cols] = n_im
                state[2 * j], state[2 * j + 1] = n_re, n_im
        for p in range(n_piece):
            st_ref[:, p * SLAB_STATE:(p + 1) * SLAB_STATE] = state[p]

    def readout(buf):
        for j in range(N_SLABS):
            s_re = buf[:, j * width:j * width + SLAB_STATE].astype(BF16)
            s_im = buf[:, j * width + SLAB_STATE:(j + 1) * width].astype(BF16)
            y_ref[0, :, j * LANES:(j + 1) * LANES] = _dot(s_re, cre_ref[0, j]) - _dot(s_im, cim_ref[0, j])

    @pl.when(i == 0)
    def _():
        st_ref[...] = jnp.zeros_like(st_ref)
        input_map(ucur_ref, bufs[0])
        bufs[SCAN_BUFFERS - 1][...] = jnp.zeros_like(bufs[SCAN_BUFFERS - 1])

    for r in range(SCAN_BUFFERS):
        @pl.when(lax.rem(i, SCAN_BUFFERS) == r)
        def _(r=r):
            input_map(unext_ref, bufs[(r + 1) % SCAN_BUFFERS])
            recurrence(bufs[r])
            readout(bufs[(r + SCAN_BUFFERS - 1) % SCAN_BUFFERS])


def _s5_scan(u_tm, w_b, c_re, c_im, a_re, a_im):
    dir_map = lambda d, i: (d, 0, 0, 0)
    vec_map = lambda d, i: (d, 0, 0)
    last = SCAN_CHUNKS - 1
    return pl.pallas_call(
        _s5_scan_kernel,
        out_shape=jax.ShapeDtypeStruct((2, N_TOK * BATCH, SSM_WIDTH), F32),
        grid=(2, SCAN_CHUNKS + 1),
        in_specs=[
            pl.BlockSpec((SCAN_ROWS, SSM_WIDTH), lambda d, i: (_scan_chunk(d, jnp.minimum(i, last)), 0)),
            pl.BlockSpec((SCAN_ROWS, SSM_WIDTH), lambda d, i: (_scan_chunk(d, jnp.minimum(i + 1, last)), 0)),
            pl.BlockSpec((1, N_SLABS, LANES, 2 * SLAB_STATE), dir_map),
            pl.BlockSpec((1, N_SLABS, SLAB_STATE, LANES), dir_map),
            pl.BlockSpec((1, N_SLABS, SLAB_STATE, LANES), dir_map),
            pl.BlockSpec((1, BATCH, N_STATE), vec_map),
            pl.BlockSpec((1, BATCH, N_STATE), vec_map),
        ],
        out_specs=pl.BlockSpec((1, SCAN_ROWS, SSM_WIDTH),
                               lambda d, i: (d, _scan_chunk(d, jnp.maximum(i - 1, 0)), 0)),
        scratch_shapes=[pltpu.VMEM((SCAN_ROWS, 2 * N_STATE), F32) for _ in range(SCAN_BUFFERS)] + [
            pltpu.VMEM((BATCH, 2 * N_STATE), F32),
        ],
        compiler_params=_params("arbitrary", "arbitrary"),
        name="s5_scan",
    )(u_tm, u_tm, w_b, c_re, c_im, a_re, a_im)


HALO_BLOCKS = TIME_BLOCK // SUBLANES


def _even_out_kernel(y_ref, u_ref, p_ref, pprev_ref, pnext_ref, q_ref, zs_ref, ctx_ref, x_ref, mod_ref,
                     d_ref, gw_ref, gb_ref, cw_ref, wo_ref, perm_ref, o_ref):
    i = pl.program_id(0)
    flat = lambda ref: ref[...].reshape(TOK_BLOCK, ref.shape[-1])
    y_hi, y_lo = _split_bf16(y_ref[0] + y_ref[1])
    y = _dot(perm_ref[...], y_hi) + _dot(perm_ref[...], y_lo) + d_ref[...] * flat(u_ref)
    g = jax.nn.gelu(y)
    z = _dot(g.astype(BF16), gw_ref[...]) + gb_ref[...]
    ssm = g * jax.nn.sigmoid(z) * flat(zs_ref)

    p = flat(p_ref)
    t_in_block = lax.broadcasted_iota(jnp.int32, p.shape, 0) % TIME_BLOCK
    prev_ok = jnp.logical_and(i != 0, i != CTX_TIME_BLOCKS).astype(F32)
    next_ok = jnp.logical_and(i != CTX_TIME_BLOCKS - 1, i != TIME_BLOCKS - 1).astype(F32)
    halo_shape = (BATCH, TIME_BLOCK, CONV_WIDTH)
    halo_prev = jnp.broadcast_to(pprev_ref[:, SUBLANES - 1:SUBLANES, :] * prev_ok, halo_shape)
    halo_next = jnp.broadcast_to(pnext_ref[:, 0:1, :] * next_ok, halo_shape)
    p_prev = jnp.where(t_in_block == 0, halo_prev.reshape(p.shape), pltpu.roll(p, 1, 0))
    p_next = jnp.where(t_in_block == TIME_BLOCK - 1, halo_next.reshape(p.shape), pltpu.roll(p, TOK_BLOCK - 1, 0))
    cw = cw_ref[...]
    conv = flat(q_ref) * (p_prev * cw[0:1, :] + p * cw[1:2, :] + p_next * cw[2:3, :])

    out = _dot(conv.astype(BF16), wo_ref[0:CONV_WIDTH, :]) + _dot(ssm.astype(BF16), wo_ref[CONV_WIDTH:, :])
    gate = mod_ref[0][:, 2:3, :]
    o_ref[...] = _pick_hidden_time(ctx_ref, x_ref) + gate * out.reshape(BATCH, TIME_BLOCK, D_MODEL)


def _even_out(y_tm, u, p, q, zs, ctx, x, mod, d_skip, glu_w, glu_b, conv_w, w_out, perm_t):
    halo_prev = lambda i: (0, jnp.maximum(i * HALO_BLOCKS - 1, 0), 0)
    halo_next = lambda i: (0, jnp.minimum((i + 1) * HALO_BLOCKS, N_TOK // SUBLANES - 1), 0)
    tok_spec = pl.BlockSpec((BATCH, TIME_BLOCK, CONV_WIDTH), _tok_time_map)
    halo = lambda m: pl.BlockSpec((BATCH, SUBLANES, CONV_WIDTH), m)
    return pl.pallas_call(
        _even_out_kernel,
        out_shape=jax.ShapeDtypeStruct((BATCH, N_TOK, D_MODEL), F32),
        grid=(TIME_BLOCKS,),
        in_specs=[
            pl.BlockSpec((2, TOK_BLOCK, SSM_WIDTH), lambda i: (0, i, 0)),
            tok_spec, tok_spec, halo(halo_prev), halo(halo_next), tok_spec, tok_spec,
            pl.BlockSpec((BATCH, TIME_BLOCK, D_MODEL), _ctx_time_map),
            pl.BlockSpec((BATCH, TIME_BLOCK, D_MODEL), _lat_time_map),
            pl.BlockSpec((1, BATCH, 3, D_MODEL), _mod_time_map),
            pl.BlockSpec((1, SSM_WIDTH), _const1),
            pl.BlockSpec((SSM_WIDTH, SSM_WIDTH), _const1),
            pl.BlockSpec((1, SSM_WIDTH), _const1),
            pl.BlockSpec((3, CONV_WIDTH), _const1),
            pl.BlockSpec((CONV_WIDTH + SSM_WIDTH, D_MODEL), _const1),
            pl.BlockSpec((TOK_BLOCK, TOK_BLOCK), _const1),
        ],
        out_specs=pl.BlockSpec((BATCH, TIME_BLOCK, D_MODEL), _tok_time_map),
        compiler_params=_params("arbitrary"),
        name="even_out",
    )(y_tm, u, p, p, p, q, zs, ctx, x, mod, d_skip, glu_w, glu_b, conv_w, w_out, perm_t)


def _rope(v, cos_t, sin_lo, sin_hi):
    half = QK_ROPE // 2
    return v * cos_t + pltpu.roll(v, LANES - half, 1) * sin_lo + pltpu.roll(v, half, 1) * sin_hi


_NT = (((1,), (1,)), ((), ()))


def _mla_in_kernel(h_ref, mod_ref, nw_ref, win_ref, qan_ref, wuqt_ref, kvan_ref, wuk_ref, wuvt_ref,
                   gqt_ref, gkn_ref, gkr_ref, segk_ref, cos_ref, slo_ref, shi_ref, cost_ref, sint_ref,
                   q_ref, k_ref, vt_ref, gate_ref):
    a = _modulated_norm(h_ref[0], nw_ref[...], mod_ref[0])
    r = _dot(a.astype(BF16), win_ref[...])
    cq = r[:, :Q_LORA]
    ckv = r[:, Q_LORA:Q_LORA + KV_LORA]
    kr = r[:, Q_LORA + KV_LORA:Q_LORA + KV_LORA + LANES]
    gate = r[:, Q_LORA + KV_LORA + LANES:]
    gate_ref[0] = gate * jax.nn.sigmoid(gate)

    cqn = _rms(cq, qan_ref[...]).astype(BF16)
    ckvn = _rms(ckv, kvan_ref[...]).astype(BF16)
    qt = lax.dot_general(wuqt_ref[...], cqn, _NT, preferred_element_type=F32)
    kf = _dot(ckvn, wuk_ref[...])
    vt = lax.dot_general(wuvt_ref[...], ckvn, _NT, preferred_element_type=F32)
    ones = jnp.ones((MLA_HEADS, V_ROWS - V_HEAD, TOK_BLOCK), BF16)
    vt_ref[0, :, 0] = jnp.concatenate([vt.astype(BF16).reshape(MLA_HEADS, V_HEAD, TOK_BLOCK), ones], axis=1)

    half = QK_ROPE // 2
    cos_q, sin_q = cost_ref[...], sint_ref[...]
    g_nope, g_rope = gqt_ref[0:QK_NOPE, :], gqt_ref[QK_NOPE:QK_DIM, :]
    pad_rows = jnp.zeros((HEAD_PAD - QK_DIM, TOK_BLOCK), F32)
    for h in range(MLA_HEADS):
        qh = qt[h * HEAD_PAD:(h + 1) * HEAD_PAD, :]
        sq = qh * qh
        ms_nope = jnp.sum(sq[0:QK_NOPE], axis=0, keepdims=True) * (1.0 / QK_NOPE)
        ms_rope = jnp.sum(sq[QK_NOPE:QK_DIM], axis=0, keepdims=True) * (1.0 / QK_ROPE)
        nope = qh[0:QK_NOPE] * lax.rsqrt(ms_nope + EPS) * g_nope
        rope = qh[QK_NOPE:QK_DIM] * lax.rsqrt(ms_rope + EPS) * g_rope
        x1, x2 = rope[0:half], rope[half:]
        tile = jnp.concatenate([nope, x1 * cos_q - x2 * sin_q, x1 * sin_q + x2 * cos_q, pad_rows], axis=0)
        q_ref[0, h, 0] = (tile * SOFTMAX_SCALE).astype(BF16)

    cos_t, sin_lo, sin_hi = cos_ref[...], slo_ref[...], shi_ref[...]
    kr_ms = jnp.sum(kr * kr, axis=-1, keepdims=True) * (1.0 / QK_ROPE)
    k_rope = _rope(kr * lax.rsqrt(kr_ms + EPS) * gkr_ref[...], cos_t, sin_lo, sin_hi)
    for j in range(MLA_HEADS * HEAD_PAD // SEG_TILE):
        kj = kf[:, j * SEG_TILE:(j + 1) * SEG_TILE]
        k_ms = _dot((kj * kj).astype(BF16), segk_ref[...]) * (1.0 / QK_NOPE)
        kn = kj * lax.rsqrt(k_ms + EPS) * gkn_ref[...]
        for i in range(SEG_TILE // HEAD_PAD):
            k_ref[0, j * (SEG_TILE // HEAD_PAD) + i] = (kn[:, i * HEAD_PAD:(i + 1) * HEAD_PAD] + k_rope).astype(BF16)


def _mla_in(h, mod, norm_w, w_in, q_a_norm, w_uq_t, kv_a_norm, w_uk, w_uv_t, gq_t, gkn, gkr, seg_k,
            cos_t, sin_lo, sin_hi, cos_q, sin_q):
    head = jax.ShapeDtypeStruct((BATCH, MLA_HEADS, N_TOK, HEAD_PAD), BF16)
    head_spec = pl.BlockSpec((1, MLA_HEADS, TOK_BLOCK, HEAD_PAD), lambda b, t: (b, 0, t, 0))
    per_qb = Q_BLOCK // TOK_BLOCK
    q_lat = jax.ShapeDtypeStruct((BATCH, MLA_HEADS, SEQ // Q_BLOCK, HEAD_PAD, Q_BLOCK), BF16)

    def q_map(b, t):
        lat = jnp.maximum(t - CTX_BLOCKS, 0)
        return (b, 0, lat // per_qb, 0, lat % per_qb)
    q_spec = pl.BlockSpec((1, MLA_HEADS, 1, HEAD_PAD, TOK_BLOCK), q_map)
    vt = jax.ShapeDtypeStruct((BATCH, MLA_HEADS, N_TOK // KV_CHUNK, V_ROWS, KV_CHUNK), BF16)
    vt_spec = pl.BlockSpec((1, MLA_HEADS, 1, V_ROWS, KV_CHUNK), lambda b, t: (b, 0, t, 0, 0))
    tab_spec = pl.BlockSpec((TOK_BLOCK, HEAD_PAD), lambda b, t: (t, 0))
    tab_t_spec = pl.BlockSpec((QK_ROPE // 2, TOK_BLOCK), lambda b, t: (0, t))
    row = lambda n: pl.BlockSpec((1, n), _const2)
    full = lambda *shape: pl.BlockSpec(shape, _const2)
    return pl.pallas_call(
        _mla_in_kernel,
        out_shape=(q_lat, head, vt, jax.ShapeDtypeStruct((BATCH, N_TOK, MLA_MIX), F32)),
        grid=(BATCH, N_BLOCKS),
        in_specs=[
            pl.BlockSpec((1, TOK_BLOCK, D_MODEL), _tok_map),
            pl.BlockSpec((1, 3, D_MODEL), _mod_map),
            row(D_MODEL),
            full(D_MODEL, ODD_IN_PAD),
            row(Q_LORA),
            full(MLA_HEADS * HEAD_PAD, Q_LORA),
            row(KV_LORA),
            full(KV_LORA, MLA_HEADS * HEAD_PAD),
            full(MLA_MIX, KV_LORA),
            full(HEAD_PAD, TOK_BLOCK), row(SEG_TILE), row(HEAD_PAD),
            full(SEG_TILE, SEG_TILE),
            tab_spec, tab_spec, tab_spec, tab_t_spec, tab_t_spec,
        ],
        out_specs=(q_spec, head_spec, vt_spec, pl.BlockSpec((1, TOK_BLOCK, MLA_MIX), _tok_map)),
        compiler_params=_params("arbitrary", "arbitrary"),
        name="mla_in",
    )(h, mod, norm_w, w_in, q_a_norm, w_uq_t, kv_a_norm, w_uk, w_uv_t, gq_t, gkn, gkr, seg_k,
      cos_t, sin_lo, sin_hi, cos_q, sin_q)


def _attention_kernel(q_ref, k_ref, vt_ref, o_ref, s_ref, ot_ref):
    n_chunks = N_TOK // KV_CHUNK
    groups = KV_CHUNK // SUBLANES
    n_qb = SEQ // Q_BLOCK
    assert HEADS_PER_STEP % 2 == 0

    def q_rows(qb):
        return pl.ds(pl.multiple_of(qb * Q_BLOCK, Q_BLOCK), Q_BLOCK)

    def chunk(c):
        return slice(c * KV_CHUNK, (c + 1) * KV_CHUNK)

    def stages(score, value, mx_prev):
        mx = m_val = None
        if value is not None:
            m_val = jnp.max(mx_prev, axis=0, keepdims=True)
        for c in range(n_chunks):
            if score is not None:
                qb, hh = score
                sc = _dot(k_ref[0, hh, chunk(c), :], q_ref[0, hh, qb])
                s_ref[hh % 2, chunk(c), :] = sc
                cm = jnp.max(sc.reshape(groups, SUBLANES, Q_BLOCK), axis=0)
                mx = cm if mx is None else jnp.maximum(mx, cm)
            if value is not None:
                qb, hh = value
                e = jnp.exp2(s_ref[hh % 2, chunk(c), :] - m_val)
                pv = _dot(vt_ref[0, hh, c], e.astype(BF16))
                acc = pv if c == 0 else acc + pv
        if value is not None:
            qb, hh = value
            inv = 1.0 / acc[V_HEAD:V_HEAD + 1, :]
            ot_ref[hh * V_HEAD:(hh + 1) * V_HEAD, :] = acc[0:V_HEAD, :] * inv
            if hh == HEADS_PER_STEP - 1:
                o_ref[0, q_rows(qb), :] = ot_ref[...].T
        return mx

    def within_block(qb, mx):
        for hh in range(HEADS_PER_STEP - 1):
            mx = stages((qb, hh + 1), (qb, hh), mx)
        return mx

    def block(qb, mx):
        return stages((qb + 1, 0), (qb, HEADS_PER_STEP - 1), within_block(qb, mx))

    mx = stages((0, 0), None, None)
    mx = lax.fori_loop(0, n_qb - 1, block, mx)
    mx = within_block(n_qb - 1, mx)
    stages(None, (n_qb - 1, HEADS_PER_STEP - 1), mx)


def _attention(q, k, vt):
    pairs = MLA_HEADS // HEADS_PER_STEP
    pair_map = lambda b, hp: (b, hp, 0, 0)
    return pl.pallas_call(
        _attention_kernel,
        out_shape=jax.ShapeDtypeStruct((BATCH, SEQ, MLA_MIX), F32),
        grid=(BATCH, pairs),
        in_specs=[
            pl.BlockSpec((1, HEADS_PER_STEP, SEQ // Q_BLOCK, HEAD_PAD, Q_BLOCK), lambda b, hp: (b, hp, 0, 0, 0)),
            pl.BlockSpec((1, HEADS_PER_STEP, N_TOK, HEAD_PAD), pair_map),
            pl.BlockSpec((1, HEADS_PER_STEP, N_TOK // KV_CHUNK, V_ROWS, KV_CHUNK), lambda b, hp: (b, hp, 0, 0, 0)),
        ],
        out_specs=pl.BlockSpec((1, SEQ, HEADS_PER_STEP * V_HEAD), lambda b, hp: (b, 0, hp)),
        scratch_shapes=[
            pltpu.VMEM((2, N_TOK, Q_BLOCK), F32),
            pltpu.VMEM((HEADS_PER_STEP * V_HEAD, Q_BLOCK), F32),
        ],
        compiler_params=_params("arbitrary", "arbitrary"),
        name="attention",
    )(q, k, vt)


def _mla_out_kernel(att_ref, gate_ref, h_ref, mod_ref, wo_ref, o_ref):
    mix = (att_ref[0] * gate_ref[0]).astype(BF16)
    o_ref[0] = h_ref[0] + mod_ref[0, 2:3, :] * _dot(mix, wo_ref[...])


def _mla_out(att, gate, h, mod, w_out):
    lat_tok = lambda b, t: (b, t + CTX_BLOCKS, 0)
    return pl.pallas_call(
        _mla_out_kernel,
        out_shape=jax.ShapeDtypeStruct((BATCH, SEQ, D_MODEL), F32),
        grid=(BATCH, SEQ // TOK_BLOCK),
        in_specs=[
            pl.BlockSpec((1, TOK_BLOCK, MLA_MIX), _tok_map),
            pl.BlockSpec((1, TOK_BLOCK, MLA_MIX), lat_tok),
            pl.BlockSpec((1, TOK_BLOCK, D_MODEL), lat_tok),
            pl.BlockSpec((1, 3, D_MODEL), lambda b, t: (b, 0, 0)),
            pl.BlockSpec((MLA_MIX, D_MODEL), _const2),
        ],
        out_specs=pl.BlockSpec((1, TOK_BLOCK, D_MODEL), _tok_map),
        compiler_params=_params("arbitrary", "arbitrary"),
        name="mla_out",
    )(att, gate, h, mod, w_out)


def _pad_heads(w, width, offset=0):
    k = w.shape[0]
    w = w.reshape(k, MLA_HEADS, width)
    return jnp.pad(w, ((0, 0), (0, 0), (offset, HEAD_PAD - width - offset))).reshape(k, MLA_HEADS * HEAD_PAD)


def _mla_weights(w_in, w_uq, w_ukv, q_norm, k_norm):
    cuts = [Q_LORA, Q_LORA + KV_LORA, Q_LORA + KV_LORA + QK_ROPE]
    w_kr = jnp.pad(w_in[:, cuts[1]:cuts[2]], ((0, 0), (QK_NOPE, LANES - QK_DIM)))
    w_in_p = jnp.concatenate([w_in[:, :cuts[1]], w_kr, w_in[:, cuts[2]:]], axis=1).astype(BF16)
    w_uq_t = _pad_heads(w_uq, QK_DIM).T.astype(BF16)
    w_kv = w_ukv.reshape(KV_LORA, MLA_HEADS, QK_NOPE + V_HEAD)
    w_uk = _pad_heads(w_kv[:, :, :QK_NOPE].reshape(KV_LORA, -1), QK_NOPE).astype(BF16)
    w_uv_t = w_kv[:, :, QK_NOPE:].reshape(KV_LORA, MLA_MIX).T.astype(BF16)
    pad_row = lambda v, off: jnp.pad(v, (off, HEAD_PAD - v.shape[0] - off)).reshape(1, HEAD_PAD)
    gq_t = jnp.broadcast_to(pad_row(q_norm, 0).reshape(HEAD_PAD, 1), (HEAD_PAD, TOK_BLOCK))
    gkn = jnp.tile(pad_row(k_norm[:QK_NOPE], 0), (1, SEG_TILE // HEAD_PAD))
    gkr = pad_row(k_norm[QK_NOPE:], QK_NOPE)
    return w_in_p, w_uq_t, w_uk, w_uv_t, gq_t, gkn, gkr


def _key_segment_matrix():
    lane = jnp.arange(SEG_TILE)
    is_nope = lane % HEAD_PAD < QK_NOPE
    same_head = (lane // HEAD_PAD)[:, None] == (lane // HEAD_PAD)[None, :]
    return (same_head & is_nope[:, None] & is_nope[None, :]).astype(BF16)


def _rope_tables():
    rows = SEQ // GRID_W
    row = jnp.repeat(jnp.arange(rows, dtype=F32), GRID_W)
    col = jnp.tile(jnp.arange(GRID_W, dtype=F32), rows)
    n_freq = QK_ROPE // 4
    inv = jnp.power(ROPE_BASE, -jnp.arange(n_freq, dtype=F32) / n_freq)
    ang = jnp.concatenate([row[:, None] * inv, col[:, None] * inv], axis=-1)
    cos, sin = jnp.cos(ang), jnp.sin(ang)
    half = QK_ROPE // 2
    zeros = jnp.zeros_like(cos)
    pad_l = jnp.ones((SEQ, QK_NOPE), F32)
    tail = jnp.zeros((SEQ, HEAD_PAD - QK_DIM), F32)
    cos_t = jnp.concatenate([pad_l, cos, cos, tail], axis=-1)
    sin_lo = jnp.concatenate([0 * pad_l, -sin, zeros, tail], axis=-1)
    sin_hi = jnp.concatenate([0 * pad_l, zeros, sin, tail], axis=-1)
    ctx_cos = jnp.concatenate([jnp.ones((CTX_LEN, QK_DIM), F32), jnp.zeros((CTX_LEN, HEAD_PAD - QK_DIM), F32)], -1)
    ctx_zero = jnp.zeros((CTX_LEN, HEAD_PAD), F32)
    cos_q = jnp.concatenate([jnp.ones((half, CTX_LEN), F32), cos.T], axis=1)
    sin_q = jnp.concatenate([jnp.zeros((half, CTX_LEN), F32), sin.T], axis=1)
    return (jnp.concatenate([ctx_cos, cos_t], 0), jnp.concatenate([ctx_zero, sin_lo], 0),
            jnp.concatenate([ctx_zero, sin_hi], 0), cos_q, sin_q)


def kernel(x, c, ctx, c_ctx, mod_w, mod_b, norm_w, e_w_in, e_conv_w, e_lam_re, e_lam_im, e_log_step, e_b_re, e_b_im, e_c_re, e_c_im, e_d, e_glu_w, e_glu_b, e_w_out, o_w_in, o_q_a_norm, o_w_uq, o_kv_a_norm, o_w_ukv, o_q_norm, o_k_norm, o_w_out):
    cond = jnp.zeros((COND_ROWS, D_MODEL), F32).at[:BATCH].set(c).at[CTX_ROW].set(c_ctx)
    mod = _modulation(cond, mod_w, mod_b)[:, :CTX_ROW + 1].reshape(2, CTX_ROW + 1, 3, D_MODEL)

    mod_even = jnp.stack([mod[0, :BATCH], jnp.broadcast_to(mod[0, CTX_ROW], (BATCH, 3, D_MODEL))])
    perm = _time_major_perm()
    p, q, zs, u, u_tm = _even_in(ctx, x, mod_even, norm_w[0:1], e_w_in[0].astype(BF16), perm)
    a_re, a_im, bbr, bbi = _s5_discretize(e_lam_re[0], e_lam_im[0], e_log_step[0], e_b_re[0], e_b_im[0])
    w_b, c_re, c_im = _s5_matrices(bbr, bbi, e_c_re[0], e_c_im[0])
    bcast = lambda a: jnp.broadcast_to(a[:, None, :], (2, BATCH, N_STATE))
    y = _s5_scan(u_tm, w_b, c_re, c_im, bcast(a_re), bcast(a_im))
    h1 = _even_out(y, u, p, q, zs, ctx, x, mod_even, e_d[0:1], e_glu_w[0].astype(BF16), e_glu_b[0:1],
                   e_conv_w[0], e_w_out[0].astype(BF16), perm.T)

    w_in_p, w_uq_t, w_uk, w_uv_t, gq_t, gkn, gkr = _mla_weights(o_w_in[0], o_w_uq[0], o_w_ukv[0],
                                                                o_q_norm[0], o_k_norm[0])
    cos_t, sin_lo, sin_hi, cos_q, sin_q = _rope_tables()
    qh, kh, vh, gate = _mla_in(h1, mod[1], norm_w[1:2], w_in_p, o_q_a_norm[0:1], w_uq_t, o_kv_a_norm[0:1],
                               w_uk, w_uv_t, gq_t, gkn, gkr, _key_segment_matrix(),
                               cos_t, sin_lo, sin_hi, cos_q, sin_q)
    att = _attention(qh, kh, vh)
    return _mla_out(att, gate, h1, mod[1], o_w_out[0].astype(BF16))
```

```python
import functools
import math

import jax
import jax.numpy as jnp
from jax import lax
from jax.experimental import pallas as pl
from jax.experimental.pallas import tpu as pltpu

F32 = jnp.float32
BF16 = jnp.bfloat16

D_MODEL = 1024
BATCH = 8
SEQ = 4096
CTX_LEN = 256
GRID_W = 64
EPS = 1e-6
CONV_WIDTH = 512
SSM_WIDTH = 512
SSM_GROUP = 16
SSM_GROUPS = SSM_WIDTH // SSM_GROUP
SSM_STATE = 64
MLA_HEADS = 16
QK_NOPE = 64
QK_ROPE = 32
QK_DIM = QK_NOPE + QK_ROPE
V_HEAD = 64
Q_LORA = 384
KV_LORA = 256
MLA_MIX = MLA_HEADS * V_HEAD
ROPE_BASE = 10000.0

LANES = 128
SUBLANES = 8
VMEM_LIMIT_BYTES = 56 * 1024 * 1024

N_TOK = CTX_LEN + SEQ
TOK_BLOCK = 256
N_BLOCKS = N_TOK // TOK_BLOCK
CTX_BLOCKS = CTX_LEN // TOK_BLOCK
assert CTX_BLOCKS == 1 and N_TOK % TOK_BLOCK == 0
COND_ROWS = 2 * SUBLANES
CTX_ROW = BATCH
EVEN_IN = 4 * CONV_WIDTH + 2 * SSM_WIDTH

SCAN_STEPS = 64
SCAN_ROWS = SCAN_STEPS * BATCH
SCAN_CHUNKS = N_TOK // SCAN_STEPS
SCAN_CTX_CHUNKS = CTX_LEN // SCAN_STEPS
SLAB_GROUPS = LANES // SSM_GROUP
N_SLABS = SSM_GROUPS // SLAB_GROUPS
SLAB_STATE = SLAB_GROUPS * SSM_STATE
N_STATE = SSM_GROUPS * SSM_STATE

HEAD_PAD = LANES
SEG_TILE = 2 * LANES
Q_BLOCK = 512
HEADS_PER_STEP = 4
KV_CHUNK = 256
V_ROWS = V_HEAD + 2 * SUBLANES
assert KV_CHUNK == TOK_BLOCK
ODD_IN_PAD = Q_LORA + KV_LORA + LANES + MLA_MIX
SOFTMAX_SCALE = QK_DIM ** -0.5 * math.log2(math.e)


def _params(*sem):
    return pltpu.CompilerParams(dimension_semantics=sem, vmem_limit_bytes=VMEM_LIMIT_BYTES)


def _dot(a, b):
    return jnp.dot(a, b, preferred_element_type=F32)


def _split_bf16(a):
    hi = a.astype(BF16)
    lo = (a - hi.astype(F32)).astype(BF16)
    return hi, lo


def _dot3(a, b):
    ah, al = _split_bf16(a)
    bh, bl = _split_bf16(b)
    return _dot(ah, bh) + (_dot(ah, bl) + _dot(al, bh))


def _rms(x, w):
    return x * lax.rsqrt(jnp.mean(x * x, axis=-1, keepdims=True) + EPS) * w


def _modulated_norm(h, norm_w, mod):
    return _rms(h, norm_w) * (1.0 + mod[1:2, :]) + mod[0:1, :]


def _mod_kernel(cond_ref, w_ref, b_ref, o_ref):
    cond = cond_ref[...]
    o_ref[0] = _dot3(cond * jax.nn.sigmoid(cond), w_ref[0]) + b_ref[0]


def _modulation(cond, mod_w, mod_b):
    depth = mod_w.shape[0]
    n_col = 3 * D_MODEL // D_MODEL
    return pl.pallas_call(
        _mod_kernel,
        out_shape=jax.ShapeDtypeStruct((depth, COND_ROWS, 3 * D_MODEL), F32),
        grid=(depth, n_col),
        in_specs=[
            pl.BlockSpec((COND_ROWS, D_MODEL), lambda i, j: (0, 0)),
            pl.BlockSpec((1, D_MODEL, D_MODEL), lambda i, j: (i, 0, j)),
            pl.BlockSpec((1, 1, D_MODEL), lambda i, j: (i, 0, j)),
        ],
        out_specs=pl.BlockSpec((1, COND_ROWS, D_MODEL), lambda i, j: (i, 0, j)),
        compiler_params=_params("arbitrary", "arbitrary"),
        name="modulation",
    )(cond, mod_w, mod_b.reshape(depth, 1, 3 * D_MODEL))


def _tok_map(b, t):
    return (b, t, 0)


def _mod_map(b, t):
    return (jnp.where(t < CTX_BLOCKS, CTX_ROW, b), 0, 0)


def _const2(b, t):
    return (0, 0)


EVEN_ROWS = 512
TIME_BLOCK = EVEN_ROWS // BATCH
TIME_BLOCKS = N_TOK // TIME_BLOCK
CTX_TIME_BLOCKS = CTX_LEN // TIME_BLOCK
assert N_TOK % TIME_BLOCK == 0 and CTX_LEN % TIME_BLOCK == 0


PERM_ROWS = 256
PERM_STEPS = PERM_ROWS // BATCH
PERM_PARTS = EVEN_ROWS // PERM_ROWS


def _time_major_perm():
    r = jnp.arange(PERM_ROWS)
    src = (r % BATCH) * PERM_STEPS + r // BATCH
    return (src[:, None] == jnp.arange(PERM_ROWS)[None, :]).astype(BF16)


def _time_part(v, part):
    v3 = v.reshape(BATCH, TIME_BLOCK, v.shape[-1])
    return v3[:, part * PERM_STEPS:(part + 1) * PERM_STEPS, :].reshape(PERM_ROWS, v.shape[-1])


def _ctx_time_map(i):
    return (0, jnp.minimum(i, CTX_TIME_BLOCKS - 1), 0)


def _lat_time_map(i):
    return (0, jnp.maximum(i - CTX_TIME_BLOCKS, 0), 0)


def _tok_time_map(i):
    return (0, i, 0)


def _mod_time_map(i):
    return (jnp.where(i < CTX_TIME_BLOCKS, 1, 0), 0, 0, 0)


def _const1(i):
    return (0, 0)


def _pick_hidden_time(ctx_ref, x_ref):
    return jnp.where(pl.program_id(0) < CTX_TIME_BLOCKS, ctx_ref[...], x_ref[...])


def _even_in_kernel(ctx_ref, x_ref, mod_ref, nw_ref, w_ref, perm_ref, p_ref, q_ref, zs_ref, u_ref, utm_ref):
    h = _pick_hidden_time(ctx_ref, x_ref)
    mod = mod_ref[0]
    a = _rms(h, nw_ref[...]) * (1.0 + mod[:, 1:2, :]) + mod[:, 0:1, :]
    r = _dot(a.reshape(EVEN_ROWS, D_MODEL).astype(BF16), w_ref[...])
    cw = CONV_WIDTH
    xa, ba, ca, za = (r[:, i * cw:(i + 1) * cw] for i in range(4))
    us = r[:, 4 * cw:4 * cw + SSM_WIDTH]
    zs = r[:, 4 * cw + SSM_WIDTH:]
    blk = (BATCH, TIME_BLOCK, CONV_WIDTH)
    p_ref[...] = (ca * xa).reshape(blk)
    q_ref[...] = (ba * (za * jax.nn.sigmoid(za))).reshape(blk)
    zs_ref[...] = (zs * jax.nn.sigmoid(zs)).reshape(blk)
    u_ref[...] = us.reshape(blk)
    ub = us.astype(BF16)
    for part in range(PERM_PARTS):
        utm_ref[part * PERM_ROWS:(part + 1) * PERM_ROWS, :] = _dot(perm_ref[...], _time_part(ub, part)).astype(BF16)


def _even_in(ctx, x, mod, norm_w, w_in, perm):
    tok = jax.ShapeDtypeStruct((BATCH, N_TOK, CONV_WIDTH), F32)
    tok_spec = pl.BlockSpec((BATCH, TIME_BLOCK, CONV_WIDTH), _tok_time_map)
    return pl.pallas_call(
        _even_in_kernel,
        out_shape=(tok, tok, tok, tok, jax.ShapeDtypeStruct((N_TOK * BATCH, SSM_WIDTH), BF16)),
        grid=(TIME_BLOCKS,),
        in_specs=[
            pl.BlockSpec((BATCH, TIME_BLOCK, D_MODEL), _ctx_time_map),
            pl.BlockSpec((BATCH, TIME_BLOCK, D_MODEL), _lat_time_map),
            pl.BlockSpec((1, BATCH, 3, D_MODEL), _mod_time_map),
            pl.BlockSpec((1, D_MODEL), _const1),
            pl.BlockSpec((D_MODEL, EVEN_IN), _const1),
            pl.BlockSpec((PERM_ROWS, PERM_ROWS), _const1),
        ],
        out_specs=(tok_spec, tok_spec, tok_spec, tok_spec,
                   pl.BlockSpec((EVEN_ROWS, SSM_WIDTH), lambda i: (i, 0))),
        compiler_params=_params("arbitrary"),
        name="even_in",
    )(ctx, x, mod, norm_w, w_in, perm)


def _s5_disc_kernel(lr_ref, li_ref, ls_ref, br_ref, bi_ref, ar_ref, ai_ref, bbr_ref, bbi_ref):
    lr = lr_ref[...]
    li = li_ref[...]
    dt = jnp.exp(ls_ref[...])
    mag = jnp.exp(lr * dt)
    ar = mag * jnp.cos(li * dt)
    ai = mag * jnp.sin(li * dt)
    nr = ar - 1.0
    den = lr * lr + li * li
    fr = (nr * lr + ai * li) / den
    fi = (ai * lr - nr * li) / den
    ar_ref[...] = ar
    ai_ref[...] = ai
    br = br_ref[...]
    bi = bi_ref[...]
    bbr_ref[...] = fr[:, None, :] * br - fi[:, None, :] * bi
    bbi_ref[...] = fr[:, None, :] * bi + fi[:, None, :] * br


def _s5_discretize(lam_re, lam_im, log_step, b_re, b_im):
    flat = lambda a: a.reshape(2, N_STATE)
    chan = lambda a: jnp.transpose(a, (0, 3, 1, 2)).reshape(2, SSM_GROUP, N_STATE)
    ls = jnp.repeat(log_step, SSM_STATE, axis=-1)
    vec = jax.ShapeDtypeStruct((2, N_STATE), F32)
    mat = jax.ShapeDtypeStruct((2, SSM_GROUP, N_STATE), F32)
    return pl.pallas_call(_s5_disc_kernel, out_shape=(vec, vec, mat, mat), name="s5_discretize")(
        flat(lam_re), flat(lam_im), ls, chan(b_re), chan(b_im))


def _block_diag_groups(a):
    d, j, g, r, c = a.shape
    eye = jnp.eye(g, dtype=bool)[None, None, :, None, :, None]
    full = jnp.where(eye, a[:, :, :, :, None, :], jnp.zeros((), a.dtype))
    return full.reshape(d, j, g * r, g * c)


def _s5_matrices(bbr, bbi, c_re, c_im):
    def in_map(bb):
        a = bb.reshape(2, SSM_GROUP, N_SLABS, SLAB_GROUPS, SSM_STATE)
        return _block_diag_groups(jnp.transpose(a, (0, 2, 3, 1, 4)))
    w_b = jnp.concatenate([in_map(bbr), in_map(bbi)], axis=-1).astype(BF16)

    def out_map(cc):
        a = cc.reshape(2, N_SLABS, SLAB_GROUPS, SSM_GROUP, SSM_STATE)
        return _block_diag_groups(jnp.transpose(a, (0, 1, 2, 4, 3))).astype(BF16)
    return w_b, out_map(c_re), out_map(c_im)


def _scan_chunk(d, i):
    back = jnp.where(i < SCAN_CTX_CHUNKS, SCAN_CTX_CHUNKS - 1 - i,
                     SCAN_CHUNKS - 1 - (i - SCAN_CTX_CHUNKS))
    return jnp.where(d == 0, i, back)


SCAN_BUFFERS = 3


def _s5_scan_kernel(ucur_ref, unext_ref, wb_ref, cre_ref, cim_ref, ar_ref, ai_ref, y_ref, *scratch):
    bufs, st_ref = scratch[:SCAN_BUFFERS], scratch[SCAN_BUFFERS]
    d = pl.program_id(0)
    i = pl.program_id(1)
    width = 2 * SLAB_STATE
    n_piece = 2 * N_SLABS

    def input_map(u_ref, buf):
        ub = u_ref[...]
        for j in range(N_SLABS):
            buf[:, j * width:(j + 1) * width] = _dot(ub[:, j * LANES:(j + 1) * LANES], wb_ref[0, j])

    def recurrence(buf):
        state = [st_ref[:, p * SLAB_STATE:(p + 1) * SLAB_STATE] for p in range(n_piece)]
        for k in range(SCAN_STEPS):
            t = jnp.where(d == 0, k, SCAN_STEPS - 1 - k)
            rows = pl.ds(pl.multiple_of(t * BATCH, BATCH), BATCH)
            for j in range(N_SLABS):
                s_re, s_im = state[2 * j], state[2 * j + 1]
                a_re = ar_ref[0, :, j * SLAB_STATE:(j + 1) * SLAB_STATE]
                a_im = ai_ref[0, :, j * SLAB_STATE:(j + 1) * SLAB_STATE]
                re_cols = slice(j * width, j * width + SLAB_STATE)
                im_cols = slice(j * width + SLAB_STATE, (j + 1) * width)
                n_re = a_re * s_re - a_im * s_im + buf[rows, re_cols]
                n_im = a_re * s_im + a_im * s_re + buf[rows, im_cols]
                buf[rows, re_cols] = n_re
                buf[rows, im_cols] = n_im
                state[2 * j], state[2 * j + 1] = n_re, n_im
        for p in range(n_piece):
            st_ref[:, p * SLAB_STATE:(p + 1) * SLAB_STATE] = state[p]

    def readout(buf):
        for j in range(N_SLABS):
            s_re = buf[:, j * width:j * width + SLAB_STATE].astype(BF16)
            s_im = buf[:, j * width + SLAB_STATE:(j + 1) * width].astype(BF16)
            y_ref[0, :, j * LANES:(j + 1) * LANES] = _dot(s_re, cre_ref[0, j]) - _dot(s_im, cim_ref[0, j])

    @pl.when(i == 0)
    def _():
        st_ref[...] = jnp.zeros_like(st_ref)
        input_map(ucur_ref, bufs[0])
        bufs[SCAN_BUFFERS - 1][...] = jnp.zeros_like(bufs[SCAN_BUFFERS - 1])

    for r in range(SCAN_BUFFERS):
        @pl.when(lax.rem(i, SCAN_BUFFERS) == r)
        def _(r=r):
            input_map(unext_ref, bufs[(r + 1) % SCAN_BUFFERS])
            recurrence(bufs[r])
            readout(bufs[(r + SCAN_BUFFERS - 1) % SCAN_BUFFERS])


def _s5_scan(u_tm, w_b, c_re, c_im, a_re, a_im):
    dir_map = lambda d, i: (d, 0, 0, 0)
    vec_map = lambda d, i: (d, 0, 0)
    last = SCAN_CHUNKS - 1
    return pl.pallas_call(
        _s5_scan_kernel,
        out_shape=jax.ShapeDtypeStruct((2, N_TOK * BATCH, SSM_WIDTH), F32),
        grid=(2, SCAN_CHUNKS + 1),
        in_specs=[
            pl.BlockSpec((SCAN_ROWS, SSM_WIDTH), lambda d, i: (_scan_chunk(d, jnp.minimum(i, last)), 0)),
            pl.BlockSpec((SCAN_ROWS, SSM_WIDTH), lambda d, i: (_scan_chunk(d, jnp.minimum(i + 1, last)), 0)),
            pl.BlockSpec((1, N_SLABS, LANES, 2 * SLAB_STATE), dir_map),
            pl.BlockSpec((1, N_SLABS, SLAB_STATE, LANES), dir_map),
            pl.BlockSpec((1, N_SLABS, SLAB_STATE, LANES), dir_map),
            pl.BlockSpec((1, BATCH, N_STATE), vec_map),
            pl.BlockSpec((1, BATCH, N_STATE), vec_map),
        ],
        out_specs=pl.BlockSpec((1, SCAN_ROWS, SSM_WIDTH),
                               lambda d, i: (d, _scan_chunk(d, jnp.maximum(i - 1, 0)), 0)),
        scratch_shapes=[pltpu.VMEM((SCAN_ROWS, 2 * N_STATE), F32) for _ in range(SCAN_BUFFERS)] + [
            pltpu.VMEM((BATCH, 2 * N_STATE), F32),
        ],
        compiler_params=_params("arbitrary", "arbitrary"),
        name="s5_scan",
    )(u_tm, u_tm, w_b, c_re, c_im, a_re, a_im)


HALO_BLOCKS = TIME_BLOCK // SUBLANES


def _even_out_kernel(y_ref, u_ref, p_ref, pprev_ref, pnext_ref, q_ref, zs_ref, ctx_ref, x_ref, mod_ref,
                     d_ref, gw_ref, gb_ref, cw_ref, wo_ref, perm_ref, o_ref):
    i = pl.program_id(0)
    flat = lambda ref: ref[...].reshape(EVEN_ROWS, ref.shape[-1])
    y_hi, y_lo = _split_bf16(y_ref[0] + y_ref[1])
    parts = []
    for part in range(PERM_PARTS):
        rows = slice(part * PERM_ROWS, (part + 1) * PERM_ROWS)
        y_part = _dot(perm_ref[...], y_hi[rows]) + _dot(perm_ref[...], y_lo[rows])
        parts.append(y_part.reshape(BATCH, PERM_STEPS, SSM_WIDTH))
    y = jnp.concatenate(parts, axis=1).reshape(EVEN_ROWS, SSM_WIDTH) + d_ref[...] * flat(u_ref)
    g = jax.nn.gelu(y)
    z = _dot(g.astype(BF16), gw_ref[...]) + gb_ref[...]
    ssm = g * jax.nn.sigmoid(z) * flat(zs_ref)

    p = flat(p_ref)
    t_in_block = lax.broadcasted_iota(jnp.int32, p.shape, 0) % TIME_BLOCK
    prev_ok = jnp.logical_and(i != 0, i != CTX_TIME_BLOCKS).astype(F32)
    next_ok = jnp.logical_and(i != CTX_TIME_BLOCKS - 1, i != TIME_BLOCKS - 1).astype(F32)
    halo_shape = (BATCH, TIME_BLOCK, CONV_WIDTH)
    halo_prev = jnp.broadcast_to(pprev_ref[:, SUBLANES - 1:SUBLANES, :] * prev_ok, halo_shape)
    halo_next = jnp.broadcast_to(pnext_ref[:, 0:1, :] * next_ok, halo_shape)
    p_prev = jnp.where(t_in_block == 0, halo_prev.reshape(p.shape), pltpu.roll(p, 1, 0))
    p_next = jnp.where(t_in_block == TIME_BLOCK - 1, halo_next.reshape(p.shape), pltpu.roll(p, EVEN_ROWS - 1, 0))
    cw = cw_ref[...]
    conv = flat(q_ref) * (p_prev * cw[0:1, :] + p * cw[1:2, :] + p_next * cw[2:3, :])

    out = _dot(conv.astype(BF16), wo_ref[0:CONV_WIDTH, :]) + _dot(ssm.astype(BF16), wo_ref[CONV_WIDTH:, :])
    gate = mod_ref[0][:, 2:3, :]
    o_ref[...] = _pick_hidden_time(ctx_ref, x_ref) + gate * out.reshape(BATCH, TIME_BLOCK, D_MODEL)


def _even_out(y_tm, u, p, q, zs, ctx, x, mod, d_skip, glu_w, glu_b, conv_w, w_out, perm_t):
    halo_prev = lambda i: (0, jnp.maximum(i * HALO_BLOCKS - 1, 0), 0)
    halo_next = lambda i: (0, jnp.minimum((i + 1) * HALO_BLOCKS, N_TOK // SUBLANES - 1), 0)
    tok_spec = pl.BlockSpec((BATCH, TIME_BLOCK, CONV_WIDTH), _tok_time_map)
    halo = lambda m: pl.BlockSpec((BATCH, SUBLANES, CONV_WIDTH), m)
    return pl.pallas_call(
        _even_out_kernel,
        out_shape=jax.ShapeDtypeStruct((BATCH, N_TOK, D_MODEL), F32),
        grid=(TIME_BLOCKS,),
        in_specs=[
            pl.BlockSpec((2, EVEN_ROWS, SSM_WIDTH), lambda i: (0, i, 0)),
            tok_spec, tok_spec, halo(halo_prev), halo(halo_next), tok_spec, tok_spec,
            pl.BlockSpec((BATCH, TIME_BLOCK, D_MODEL), _ctx_time_map),
            pl.BlockSpec((BATCH, TIME_BLOCK, D_MODEL), _lat_time_map),
            pl.BlockSpec((1, BATCH, 3, D_MODEL), _mod_time_map),
            pl.BlockSpec((1, SSM_WIDTH), _const1),
            pl.BlockSpec((SSM_WIDTH, SSM_WIDTH), _const1),
            pl.BlockSpec((1, SSM_WIDTH), _const1),
            pl.BlockSpec((3, CONV_WIDTH), _const1),
            pl.BlockSpec((CONV_WIDTH + SSM_WIDTH, D_MODEL), _const1),
            pl.BlockSpec((PERM_ROWS, PERM_ROWS), _const1),
        ],
        out_specs=pl.BlockSpec((BATCH, TIME_BLOCK, D_MODEL), _tok_time_map),
        compiler_params=_params("arbitrary"),
        name="even_out",
    )(y_tm, u, p, p, p, q, zs, ctx, x, mod, d_skip, glu_w, glu_b, conv_w, w_out, perm_t)


def _rope(v, cos_t, sin_lo, sin_hi):
    half = QK_ROPE // 2
    return v * cos_t + pltpu.roll(v, LANES - half, 1) * sin_lo + pltpu.roll(v, half, 1) * sin_hi


_NT = (((1,), (1,)), ((), ()))


def _mla_in_kernel(h_ref, mod_ref, nw_ref, win_ref, qan_ref, wuqt_ref, kvan_ref, wuk_ref, wuvt_ref,
                   gqt_ref, gkn_ref, gkr_ref, segk_ref, cos_ref, slo_ref, shi_ref, cost_ref, sint_ref,
                   q_ref, k_ref, vt_ref, gate_ref):
    a = _modulated_norm(h_ref[0], nw_ref[...], mod_ref[0])
    r = _dot(a.astype(BF16), win_ref[...])
    cq = r[:, :Q_LORA]
    ckv = r[:, Q_LORA:Q_LORA + KV_LORA]
    kr = r[:, Q_LORA + KV_LORA:Q_LORA + KV_LORA + LANES]
    gate = r[:, Q_LORA + KV_LORA + LANES:]
    gate_ref[0] = gate * jax.nn.sigmoid(gate)

    cqn = _rms(cq, qan_ref[...]).astype(BF16)
    ckvn = _rms(ckv, kvan_ref[...]).astype(BF16)
    qt = lax.dot_general(wuqt_ref[...], cqn, _NT, preferred_element_type=F32)
    kf = _dot(ckvn, wuk_ref[...])
    vt = lax.dot_general(wuvt_ref[...], ckvn, _NT, preferred_element_type=F32)
    ones = jnp.ones((MLA_HEADS, V_ROWS - V_HEAD, TOK_BLOCK), BF16)
    vt_ref[0, :, 0] = jnp.concatenate([vt.astype(BF16).reshape(MLA_HEADS, V_HEAD, TOK_BLOCK), ones], axis=1)

    half = QK_ROPE // 2
    cos_q, sin_q = cost_ref[...], sint_ref[...]
    g_nope, g_rope = gqt_ref[0:QK_NOPE, :], gqt_ref[QK_NOPE:QK_DIM, :]
    pad_rows = jnp.zeros((HEAD_PAD - QK_DIM, TOK_BLOCK), F32)
    for h in range(MLA_HEADS):
        qh = qt[h * HEAD_PAD:(h + 1) * HEAD_PAD, :]
        sq = qh * qh
        ms_nope = jnp.sum(sq[0:QK_NOPE], axis=0, keepdims=True) * (1.0 / QK_NOPE)
        ms_rope = jnp.sum(sq[QK_NOPE:QK_DIM], axis=0, keepdims=True) * (1.0 / QK_ROPE)
        nope = qh[0:QK_NOPE] * lax.rsqrt(ms_nope + EPS) * g_nope
        rope = qh[QK_NOPE:QK_DIM] * lax.rsqrt(ms_rope + EPS) * g_rope
        x1, x2 = rope[0:half], rope[half:]
        tile = jnp.concatenate([nope, x1 * cos_q - x2 * sin_q, x1 * sin_q + x2 * cos_q, pad_rows], axis=0)
        q_ref[0, h, 0] = (tile * SOFTMAX_SCALE).astype(BF16)

    cos_t, sin_lo, sin_hi = cos_ref[...], slo_ref[...], shi_ref[...]
    kr_ms = jnp.sum(kr * kr, axis=-1, keepdims=True) * (1.0 / QK_ROPE)
    k_rope = _rope(kr * lax.rsqrt(kr_ms + EPS) * gkr_ref[...], cos_t, sin_lo, sin_hi)
    for j in range(MLA_HEADS * HEAD_PAD // SEG_TILE):
        kj = kf[:, j * SEG_TILE:(j + 1) * SEG_TILE]
        k_ms = _dot((kj * kj).astype(BF16), segk_ref[...]) * (1.0 / QK_NOPE)
        kn = kj * lax.rsqrt(k_ms + EPS) * gkn_ref[...]
        for i in range(SEG_TILE // HEAD_PAD):
            k_ref[0, j * (SEG_TILE // HEAD_PAD) + i] = (kn[:, i * HEAD_PAD:(i + 1) * HEAD_PAD] + k_rope).astype(BF16)


def _mla_in(h, mod, norm_w, w_in, q_a_norm, w_uq_t, kv_a_norm, w_uk, w_uv_t, gq_t, gkn, gkr, seg_k,
            cos_t, sin_lo, sin_hi, cos_q, sin_q):
    head = jax.ShapeDtypeStruct((BATCH, MLA_HEADS, N_TOK, HEAD_PAD), BF16)
    head_spec = pl.BlockSpec((1, MLA_HEADS, TOK_BLOCK, HEAD_PAD), lambda b, t: (b, 0, t, 0))
    per_qb = Q_BLOCK // TOK_BLOCK
    q_lat = jax.ShapeDtypeStruct((BATCH, MLA_HEADS, SEQ // Q_BLOCK, HEAD_PAD, Q_BLOCK), BF16)

    def q_map(b, t):
        lat = jnp.maximum(t - CTX_BLOCKS, 0)
        return (b, 0, lat // per_qb, 0, lat % per_qb)
    q_spec = pl.BlockSpec((1, MLA_HEADS, 1, HEAD_PAD, TOK_BLOCK), q_map)
    vt = jax.ShapeDtypeStruct((BATCH, MLA_HEADS, N_TOK // KV_CHUNK, V_ROWS, KV_CHUNK), BF16)
    vt_spec = pl.BlockSpec((1, MLA_HEADS, 1, V_ROWS, KV_CHUNK), lambda b, t: (b, 0, t, 0, 0))
    tab_spec = pl.BlockSpec((TOK_BLOCK, HEAD_PAD), lambda b, t: (t, 0))
    tab_t_spec = pl.BlockSpec((QK_ROPE // 2, TOK_BLOCK), lambda b, t: (0, t))
    row = lambda n: pl.BlockSpec((1, n), _const2)
    full = lambda *shape: pl.BlockSpec(shape, _const2)
    return pl.pallas_call(
        _mla_in_kernel,
        out_shape=(q_lat, head, vt, jax.ShapeDtypeStruct((BATCH, N_TOK, MLA_MIX), F32)),
        grid=(BATCH, N_BLOCKS),
        in_specs=[
            pl.BlockSpec((1, TOK_BLOCK, D_MODEL), _tok_map),
            pl.BlockSpec((1, 3, D_MODEL), _mod_map),
            row(D_MODEL),
            full(D_MODEL, ODD_IN_PAD),
            row(Q_LORA),
            full(MLA_HEADS * HEAD_PAD, Q_LORA),
            row(KV_LORA),
            full(KV_LORA, MLA_HEADS * HEAD_PAD),
            full(MLA_MIX, KV_LORA),
            full(HEAD_PAD, TOK_BLOCK), row(SEG_TILE), row(HEAD_PAD),
            full(SEG_TILE, SEG_TILE),
            tab_spec, tab_spec, tab_spec, tab_t_spec, tab_t_spec,
        ],
        out_specs=(q_spec, head_spec, vt_spec, pl.BlockSpec((1, TOK_BLOCK, MLA_MIX), _tok_map)),
        compiler_params=_params("arbitrary", "arbitrary"),
        name="mla_in",
    )(h, mod, norm_w, w_in, q_a_norm, w_uq_t, kv_a_norm, w_uk, w_uv_t, gq_t, gkn, gkr, seg_k,
      cos_t, sin_lo, sin_hi, cos_q, sin_q)


def _attention_kernel(q_ref, k_ref, vt_ref, o_ref, s_ref, ot_ref):
    n_chunks = N_TOK // KV_CHUNK
    groups = KV_CHUNK // SUBLANES
    n_qb = SEQ // Q_BLOCK
    assert HEADS_PER_STEP % 2 == 0

    def q_rows(qb):
        return pl.ds(pl.multiple_of(qb * Q_BLOCK, Q_BLOCK), Q_BLOCK)

    def chunk(c):
        return slice(c * KV_CHUNK, (c + 1) * KV_CHUNK)

    def stages(score, value, mx_prev):
        mx = m_val = None
        if value is not None:
            m_val = jnp.max(mx_prev, axis=0, keepdims=True)
        for c in range(n_chunks):
            if score is not None:
                qb, hh = score
                sc = _dot(k_ref[0, hh, chunk(c), :], q_ref[0, hh, qb])
                s_ref[hh % 2, chunk(c), :] = sc
                cm = jnp.max(sc.reshape(groups, SUBLANES, Q_BLOCK), axis=0)
                mx = cm if mx is None else jnp.maximum(mx, cm)
            if value is not None:
                qb, hh = value
                e = jnp.exp2(s_ref[hh % 2, chunk(c), :] - m_val)
                pv = _dot(vt_ref[0, hh, c], e.astype(BF16))
                acc = pv if c == 0 else acc + pv
        if value is not None:
            qb, hh = value
            inv = 1.0 / acc[V_HEAD:V_HEAD + 1, :]
            ot_ref[hh * V_HEAD:(hh + 1) * V_HEAD, :] = acc[0:V_HEAD, :] * inv
            if hh == HEADS_PER_STEP - 1:
                o_ref[0, q_rows(qb), :] = ot_ref[...].T
        return mx

    def within_block(qb, mx):
        for hh in range(HEADS_PER_STEP - 1):
            mx = stages((qb, hh + 1), (qb, hh), mx)
        return mx

    def block(qb, mx):
        return stages((qb + 1, 0), (qb, HEADS_PER_STEP - 1), within_block(qb, mx))

    mx = stages((0, 0), None, None)
    mx = lax.fori_loop(0, n_qb - 1, block, mx)
    mx = within_block(n_qb - 1, mx)
    stages(None, (n_qb - 1, HEADS_PER_STEP - 1), mx)


def _attention(q, k, vt):
    pairs = MLA_HEADS // HEADS_PER_STEP
    pair_map = lambda b, hp: (b, hp, 0, 0)
    return pl.pallas_call(
        _attention_kernel,
        out_shape=jax.ShapeDtypeStruct((BATCH, SEQ, MLA_MIX), F32),
        grid=(BATCH, pairs),
        in_specs=[
            pl.BlockSpec((1, HEADS_PER_STEP, SEQ // Q_BLOCK, HEAD_PAD, Q_BLOCK), lambda b, hp: (b, hp, 0, 0, 0)),
            pl.BlockSpec((1, HEADS_PER_STEP, N_TOK, HEAD_PAD), pair_map),
            pl.BlockSpec((1, HEADS_PER_STEP, N_TOK // KV_CHUNK, V_ROWS, KV_CHUNK), lambda b, hp: (b, hp, 0, 0, 0)),
        ],
        out_specs=pl.BlockSpec((1, SEQ, HEADS_PER_STEP * V_HEAD), lambda b, hp: (b, 0, hp)),
        scratch_shapes=[
            pltpu.VMEM((2, N_TOK, Q_BLOCK), F32),
            pltpu.VMEM((HEADS_PER_STEP * V_HEAD, Q_BLOCK), F32),
        ],
        compiler_params=_params("arbitrary", "arbitrary"),
        name="attention",
    )(q, k, vt)


def _mla_out_kernel(att_ref, gate_ref, h_ref, mod_ref, wo_ref, o_ref):
    mix = (att_ref[0] * gate_ref[0]).astype(BF16)
    o_ref[0] = h_ref[0] + mod_ref[0, 2:3, :] * _dot(mix, wo_ref[...])


def _mla_out(att, gate, h, mod, w_out):
    lat_tok = lambda b, t: (b, t + CTX_BLOCKS, 0)
    return pl.pallas_call(
        _mla_out_kernel,
        out_shape=jax.ShapeDtypeStruct((BATCH, SEQ, D_MODEL), F32),
        grid=(BATCH, SEQ // TOK_BLOCK),
        in_specs=[
            pl.BlockSpec((1, TOK_BLOCK, MLA_MIX), _tok_map),
            pl.BlockSpec((1, TOK_BLOCK, MLA_MIX), lat_tok),
            pl.BlockSpec((1, TOK_BLOCK, D_MODEL), lat_tok),
            pl.BlockSpec((1, 3, D_MODEL), lambda b, t: (b, 0, 0)),
            pl.BlockSpec((MLA_MIX, D_MODEL), _const2),
        ],
        out_specs=pl.BlockSpec((1, TOK_BLOCK, D_MODEL), _tok_map),
        compiler_params=_params("arbitrary", "arbitrary"),
        name="mla_out",
    )(att, gate, h, mod, w_out)


def _pad_heads(w, width, offset=0):
    k = w.shape[0]
    w = w.reshape(k, MLA_HEADS, width)
    return jnp.pad(w, ((0, 0), (0, 0), (offset, HEAD_PAD - width - offset))).reshape(k, MLA_HEADS * HEAD_PAD)


def _mla_weights(w_in, w_uq, w_ukv, q_norm, k_norm):
    cuts = [Q_LORA, Q_LORA + KV_LORA, Q_LORA + KV_LORA + QK_ROPE]
    w_kr = jnp.pad(w_in[:, cuts[1]:cuts[2]], ((0, 0), (QK_NOPE, LANES - QK_DIM)))
    w_in_p = jnp.concatenate([w_in[:, :cuts[1]], w_kr, w_in[:, cuts[2]:]], axis=1).astype(BF16)
    w_uq_t = _pad_heads(w_uq, QK_DIM).T.astype(BF16)
    w_kv = w_ukv.reshape(KV_LORA, MLA_HEADS, QK_NOPE + V_HEAD)
    w_uk = _pad_heads(w_kv[:, :, :QK_NOPE].reshape(KV_LORA, -1), QK_NOPE).astype(BF16)
    w_uv_t = w_kv[:, :, QK_NOPE:].reshape(KV_LORA, MLA_MIX).T.astype(BF16)
    pad_row = lambda v, off: jnp.pad(v, (off, HEAD_PAD - v.shape[0] - off)).reshape(1, HEAD_PAD)
    gq_t = jnp.broadcast_to(pad_row(q_norm, 0).reshape(HEAD_PAD, 1), (HEAD_PAD, TOK_BLOCK))
    gkn = jnp.tile(pad_row(k_norm[:QK_NOPE], 0), (1, SEG_TILE // HEAD_PAD))
    gkr = pad_row(k_norm[QK_NOPE:], QK_NOPE)
    return w_in_p, w_uq_t, w_uk, w_uv_t, gq_t, gkn, gkr


def _key_segment_matrix():
    lane = jnp.arange(SEG_TILE)
    is_nope = lane % HEAD_PAD < QK_NOPE
    same_head = (lane // HEAD_PAD)[:, None] == (lane // HEAD_PAD)[None, :]
    return (same_head & is_nope[:, None] & is_nope[None, :]).astype(BF16)


def _rope_tables():
    rows = SEQ // GRID_W
    row = jnp.repeat(jnp.arange(rows, dtype=F32), GRID_W)
    col = jnp.tile(jnp.arange(GRID_W, dtype=F32), rows)
    n_freq = QK_ROPE // 4
    inv = jnp.power(ROPE_BASE, -jnp.arange(n_freq, dtype=F32) / n_freq)
    ang = jnp.concatenate([row[:, None] * inv, col[:, None] * inv], axis=-1)
    cos, sin = jnp.cos(ang), jnp.sin(ang)
    half = QK_ROPE // 2
    zeros = jnp.zeros_like(cos)
    pad_l = jnp.ones((SEQ, QK_NOPE), F32)
    tail = jnp.zeros((SEQ, HEAD_PAD - QK_DIM), F32)
    cos_t = jnp.concatenate([pad_l, cos, cos, tail], axis=-1)
    sin_lo = jnp.concatenate([0 * pad_l, -sin, zeros, tail], axis=-1)
    sin_hi = jnp.concatenate([0 * pad_l, zeros, sin, tail], axis=-1)
    ctx_cos = jnp.concatenate([jnp.ones((CTX_LEN, QK_DIM), F32), jnp.zeros((CTX_LEN, HEAD_PAD - QK_DIM), F32)], -1)
    ctx_zero = jnp.zeros((CTX_LEN, HEAD_PAD), F32)
    cos_q = jnp.concatenate([jnp.ones((half, CTX_LEN), F32), cos.T], axis=1)
    sin_q = jnp.concatenate([jnp.zeros((half, CTX_LEN), F32), sin.T], axis=1)
    return (jnp.concatenate([ctx_cos, cos_t], 0), jnp.concatenate([ctx_zero, sin_lo], 0),
            jnp.concatenate([ctx_zero, sin_hi], 0), cos_q, sin_q)


def kernel(x, c, ctx, c_ctx, mod_w, mod_b, norm_w, e_w_in, e_conv_w, e_lam_re, e_lam_im, e_log_step, e_b_re, e_b_im, e_c_re, e_c_im, e_d, e_glu_w, e_glu_b, e_w_out, o_w_in, o_q_a_norm, o_w_uq, o_kv_a_norm, o_w_ukv, o_q_norm, o_k_norm, o_w_out):
    cond = jnp.zeros((COND_ROWS, D_MODEL), F32).at[:BATCH].set(c).at[CTX_ROW].set(c_ctx)
    mod = _modulation(cond, mod_w, mod_b)[:, :CTX_ROW + 1].reshape(2, CTX_ROW + 1, 3, D_MODEL)

    mod_even = jnp.stack([mod[0, :BATCH], jnp.broadcast_to(mod[0, CTX_ROW], (BATCH, 3, D_MODEL))])
    perm = _time_major_perm()
    p, q, zs, u, u_tm = _even_in(ctx, x, mod_even, norm_w[0:1], e_w_in[0].astype(BF16), perm)
    a_re, a_im, bbr, bbi = _s5_discretize(e_lam_re[0], e_lam_im[0], e_log_step[0], e_b_re[0], e_b_im[0])
    w_b, c_re, c_im = _s5_matrices(bbr, bbi, e_c_re[0], e_c_im[0])
    bcast = lambda a: jnp.broadcast_to(a[:, None, :], (2, BATCH, N_STATE))
    y = _s5_scan(u_tm, w_b, c_re, c_im, bcast(a_re), bcast(a_im))
    h1 = _even_out(y, u, p, q, zs, ctx, x, mod_even, e_d[0:1], e_glu_w[0].astype(BF16), e_glu_b[0:1],
                   e_conv_w[0], e_w_out[0].astype(BF16), perm.T)

    w_in_p, w_uq_t, w_uk, w_uv_t, gq_t, gkn, gkr = _mla_weights(o_w_in[0], o_w_uq[0], o_w_ukv[0],
                                                                o_q_norm[0], o_k_norm[0])
    cos_t, sin_lo, sin_hi, cos_q, sin_q = _rope_tables()
    qh, kh, vh, gate = _mla_in(h1, mod[1], norm_w[1:2], w_in_p, o_q_a_norm[0:1], w_uq_t, o_kv_a_norm[0:1],
                               w_uk, w_uv_t, gq_t, gkn, gkr, _key_segment_matrix(),
                               cos_t, sin_lo, sin_hi, cos_q, sin_q)
    att = _attention(qh, kh, vh)
    return _mla_out(att, gate, h1, mod[1], o_w_out[0].astype(BF16))
```

```python
import math

import jax
import jax.numpy as jnp
import numpy as np
from jax import lax
from jax.experimental import pallas as pl
from jax.experimental.pallas import tpu as pltpu

F32 = jnp.float32
BF16 = jnp.bfloat16

D_MODEL = 1024
BATCH = 8
SEQ = 4096
CTX_LEN = 256
GRID_W = 64
EPS = 1e-6
CONV_WIDTH = 512
SSM_WIDTH = 512
SSM_GROUP = 16
SSM_GROUPS = SSM_WIDTH // SSM_GROUP
SSM_STATE = 64
MLA_HEADS = 16
QK_NOPE = 64
QK_ROPE = 32
QK_DIM = QK_NOPE + QK_ROPE
V_HEAD = 64
Q_LORA = 384
KV_LORA = 256
MLA_MIX = MLA_HEADS * V_HEAD
ROPE_BASE = 10000.0

LANES = 128
SUBLANES = 8
VMEM_LIMIT_BYTES = 56 * 1024 * 1024

N_TOK = CTX_LEN + SEQ
TOK_BLOCK = 256
N_BLOCKS = N_TOK // TOK_BLOCK
CTX_BLOCKS = CTX_LEN // TOK_BLOCK
assert CTX_BLOCKS == 1 and N_TOK % TOK_BLOCK == 0
COND_ROWS = 2 * SUBLANES
CTX_ROW = BATCH
EVEN_IN = 4 * CONV_WIDTH + 2 * SSM_WIDTH

SCAN_STEPS = 64
SCAN_ROWS = SCAN_STEPS * BATCH
SCAN_CHUNKS = N_TOK // SCAN_STEPS
SCAN_CTX_CHUNKS = CTX_LEN // SCAN_STEPS
SLAB_GROUPS = LANES // SSM_GROUP
N_SLABS = SSM_GROUPS // SLAB_GROUPS
SLAB_STATE = SLAB_GROUPS * SSM_STATE
N_STATE = SSM_GROUPS * SSM_STATE

HEAD_PAD = LANES
SEG_TILE = 2 * LANES
Q_BLOCK = 512
HEADS_PER_STEP = 4
KV_CHUNK = 256
V_ROWS = V_HEAD + 2 * SUBLANES
assert KV_CHUNK == TOK_BLOCK
ODD_IN_PAD = Q_LORA + KV_LORA + LANES + MLA_MIX
SOFTMAX_SCALE = QK_DIM ** -0.5 * math.log2(math.e)


def _params(*sem):
    return pltpu.CompilerParams(dimension_semantics=sem, vmem_limit_bytes=VMEM_LIMIT_BYTES)


def _dot(a, b):
    return jnp.dot(a, b, preferred_element_type=F32)


def _split_bf16(a):
    hi = a.astype(BF16)
    lo = (a - hi.astype(F32)).astype(BF16)
    return hi, lo


def _dot3(a, b):
    ah, al = _split_bf16(a)
    bh, bl = _split_bf16(b)
    return _dot(ah, bh) + (_dot(ah, bl) + _dot(al, bh))


def _rms(x, w):
    return x * lax.rsqrt(jnp.mean(x * x, axis=-1, keepdims=True) + EPS) * w


def _modulated_norm(h, norm_w, mod):
    return _rms(h, norm_w) * (1.0 + mod[1:2, :]) + mod[0:1, :]


def _mod_kernel(cond_ref, w_ref, b_ref, o_ref):
    cond = cond_ref[...]
    o_ref[0] = _dot3(cond * jax.nn.sigmoid(cond), w_ref[0]) + b_ref[0]


def _modulation(cond, mod_w, mod_b):
    depth = mod_w.shape[0]
    n_col = 3 * D_MODEL // D_MODEL
    return pl.pallas_call(
        _mod_kernel,
        out_shape=jax.ShapeDtypeStruct((depth, COND_ROWS, 3 * D_MODEL), F32),
        grid=(depth, n_col),
        in_specs=[
            pl.BlockSpec((COND_ROWS, D_MODEL), lambda i, j: (0, 0)),
            pl.BlockSpec((1, D_MODEL, D_MODEL), lambda i, j: (i, 0, j)),
            pl.BlockSpec((1, 1, D_MODEL), lambda i, j: (i, 0, j)),
        ],
        out_specs=pl.BlockSpec((1, COND_ROWS, D_MODEL), lambda i, j: (i, 0, j)),
        compiler_params=_params("arbitrary", "arbitrary"),
        name="modulation",
    )(cond, mod_w, mod_b.reshape(depth, 1, 3 * D_MODEL))


def _tok_map(b, t):
    return (b, t, 0)


def _mod_map(b, t):
    return (jnp.where(t < CTX_BLOCKS, CTX_ROW, b), 0, 0)


def _const2(b, t):
    return (0, 0)


EVEN_ROWS = 512
TIME_BLOCK = EVEN_ROWS // BATCH
TIME_BLOCKS = N_TOK // TIME_BLOCK
CTX_TIME_BLOCKS = CTX_LEN // TIME_BLOCK
assert N_TOK % TIME_BLOCK == 0 and CTX_LEN % TIME_BLOCK == 0


PERM_ROWS = 256
PERM_STEPS = PERM_ROWS // BATCH
PERM_PARTS = EVEN_ROWS // PERM_ROWS


def _time_major_perm():
    r = np.arange(PERM_ROWS)
    src = (r % BATCH) * PERM_STEPS + r // BATCH
    perm = src[:, None] == np.arange(PERM_ROWS)[None, :]
    return jnp.asarray(perm, dtype=BF16), jnp.asarray(perm.T, dtype=BF16)


def _time_part(v, part):
    v3 = v.reshape(BATCH, TIME_BLOCK, v.shape[-1])
    return v3[:, part * PERM_STEPS:(part + 1) * PERM_STEPS, :].reshape(PERM_ROWS, v.shape[-1])


def _ctx_time_map(i):
    return (0, jnp.minimum(i, CTX_TIME_BLOCKS - 1), 0)


def _lat_time_map(i):
    return (0, jnp.maximum(i - CTX_TIME_BLOCKS, 0), 0)


def _tok_time_map(i):
    return (0, i, 0)


def _mod_time_map(i):
    return (jnp.where(i < CTX_TIME_BLOCKS, 1, 0), 0, 0, 0)


def _const1(i):
    return (0, 0)


def _pick_hidden_time(ctx_ref, x_ref):
    return jnp.where(pl.program_id(0) < CTX_TIME_BLOCKS, ctx_ref[...], x_ref[...])


def _even_in_kernel(ctx_ref, x_ref, mod_ref, nw_ref, w_ref, perm_ref, p_ref, q_ref, zs_ref, u_ref, utm_ref):
    h = _pick_hidden_time(ctx_ref, x_ref)
    mod = mod_ref[0]
    a = _rms(h, nw_ref[...]) * (1.0 + mod[:, 1:2, :]) + mod[:, 0:1, :]
    r = _dot(a.reshape(EVEN_ROWS, D_MODEL).astype(BF16), w_ref[...])
    cw = CONV_WIDTH
    xa, ba, ca, za = (r[:, i * cw:(i + 1) * cw] for i in range(4))
    us = r[:, 4 * cw:4 * cw + SSM_WIDTH]
    zs = r[:, 4 * cw + SSM_WIDTH:]
    blk = (BATCH, TIME_BLOCK, CONV_WIDTH)
    p_ref[...] = (ca * xa).reshape(blk)
    q_ref[...] = (ba * (za * jax.nn.sigmoid(za))).reshape(blk)
    zs_ref[...] = (zs * jax.nn.sigmoid(zs)).reshape(blk)
    u_ref[...] = us.reshape(blk)
    ub = us.astype(BF16)
    for part in range(PERM_PARTS):
        utm_ref[part * PERM_ROWS:(part + 1) * PERM_ROWS, :] = _dot(perm_ref[...], _time_part(ub, part)).astype(BF16)


def _even_in(ctx, x, mod, norm_w, w_in, perm):
    tok = jax.ShapeDtypeStruct((BATCH, N_TOK, CONV_WIDTH), F32)
    tok_spec = pl.BlockSpec((BATCH, TIME_BLOCK, CONV_WIDTH), _tok_time_map)
    return pl.pallas_call(
        _even_in_kernel,
        out_shape=(tok, tok, tok, tok, jax.ShapeDtypeStruct((N_TOK * BATCH, SSM_WIDTH), BF16)),
        grid=(TIME_BLOCKS,),
        in_specs=[
            pl.BlockSpec((BATCH, TIME_BLOCK, D_MODEL), _ctx_time_map),
            pl.BlockSpec((BATCH, TIME_BLOCK, D_MODEL), _lat_time_map),
            pl.BlockSpec((1, BATCH, 3, D_MODEL), _mod_time_map),
            pl.BlockSpec((1, D_MODEL), _const1),
            pl.BlockSpec((D_MODEL, EVEN_IN), _const1),
            pl.BlockSpec((PERM_ROWS, PERM_ROWS), _const1),
        ],
        out_specs=(tok_spec, tok_spec, tok_spec, tok_spec,
                   pl.BlockSpec((EVEN_ROWS, SSM_WIDTH), lambda i: (i, 0))),
        compiler_params=_params("arbitrary"),
        name="even_in",
    )(ctx, x, mod, norm_w, w_in, perm)


def _s5_disc_kernel(lr_ref, li_ref, ls_ref, br_ref, bi_ref, ar_ref, ai_ref, bbr_ref, bbi_ref):
    lr = lr_ref[...]
    li = li_ref[...]
    dt = jnp.exp(ls_ref[...])
    mag = jnp.exp(lr * dt)
    ar = mag * jnp.cos(li * dt)
    ai = mag * jnp.sin(li * dt)
    nr = ar - 1.0
    den = lr * lr + li * li
    fr = (nr * lr + ai * li) / den
    fi = (ai * lr - nr * li) / den
    ar_ref[...] = ar
    ai_ref[...] = ai
    br = br_ref[...]
    bi = bi_ref[...]
    bbr_ref[...] = fr[:, None, :] * br - fi[:, None, :] * bi
    bbi_ref[...] = fr[:, None, :] * bi + fi[:, None, :] * br


def _s5_discretize(lam_re, lam_im, log_step, b_re, b_im):
    flat = lambda a: a.reshape(2, N_STATE)
    chan = lambda a: jnp.transpose(a, (0, 3, 1, 2)).reshape(2, SSM_GROUP, N_STATE)
    ls = jnp.repeat(log_step, SSM_STATE, axis=-1)
    vec = jax.ShapeDtypeStruct((2, N_STATE), F32)
    mat = jax.ShapeDtypeStruct((2, SSM_GROUP, N_STATE), F32)
    return pl.pallas_call(_s5_disc_kernel, out_shape=(vec, vec, mat, mat), name="s5_discretize")(
        flat(lam_re), flat(lam_im), ls, chan(b_re), chan(b_im))


def _block_diag_groups(a):
    d, j, g, r, c = a.shape
    eye = jnp.eye(g, dtype=bool)[None, None, :, None, :, None]
    full = jnp.where(eye, a[:, :, :, :, None, :], jnp.zeros((), a.dtype))
    return full.reshape(d, j, g * r, g * c)


def _s5_matrices(bbr, bbi, c_re, c_im):
    def in_map(bb):
        a = bb.reshape(2, SSM_GROUP, N_SLABS, SLAB_GROUPS, SSM_STATE)
        return _block_diag_groups(jnp.transpose(a, (0, 2, 3, 1, 4)))
    w_b = jnp.concatenate([in_map(bbr), in_map(bbi)], axis=-1).astype(BF16)

    def out_map(cc):
        a = cc.reshape(2, N_SLABS, SLAB_GROUPS, SSM_GROUP, SSM_STATE)
        return _block_diag_groups(jnp.transpose(a, (0, 1, 2, 4, 3))).astype(BF16)
    return w_b, out_map(c_re), out_map(c_im)


def _scan_chunk(d, i):
    back = jnp.where(i < SCAN_CTX_CHUNKS, SCAN_CTX_CHUNKS - 1 - i,
                     SCAN_CHUNKS - 1 - (i - SCAN_CTX_CHUNKS))
    return jnp.where(d == 0, i, back)


SCAN_BUFFERS = 3


def _s5_scan_kernel(ucur_ref, unext_ref, wb_ref, cre_ref, cim_ref, ar_ref, ai_ref, y_ref, *scratch):
    bufs, st_ref = scratch[:SCAN_BUFFERS], scratch[SCAN_BUFFERS]
    d = pl.program_id(0)
    i = pl.program_id(1)
    width = 2 * SLAB_STATE
    n_piece = 2 * N_SLABS

    def input_map(u_ref, buf):
        ub = u_ref[...]
        for j in range(N_SLABS):
            buf[:, j * width:(j + 1) * width] = _dot(ub[:, j * LANES:(j + 1) * LANES], wb_ref[0, j])

    def recurrence(buf):
        state = [st_ref[:, p * SLAB_STATE:(p + 1) * SLAB_STATE] for p in range(n_piece)]
        for k in range(SCAN_STEPS):
            t = jnp.where(d == 0, k, SCAN_STEPS - 1 - k)
            rows = pl.ds(pl.multiple_of(t * BATCH, BATCH), BATCH)
            for j in range(N_SLABS):
                s_re, s_im = state[2 * j], state[2 * j + 1]
                a_re = ar_ref[0, :, j * SLAB_STATE:(j + 1) * SLAB_STATE]
                a_im = ai_ref[0, :, j * SLAB_STATE:(j + 1) * SLAB_STATE]
                re_cols = slice(j * width, j * width + SLAB_STATE)
                im_cols = slice(j * width + SLAB_STATE, (j + 1) * width)
                n_re = a_re * s_re - a_im * s_im + buf[rows, re_cols]
                n_im = a_re * s_im + a_im * s_re + buf[rows, im_cols]
                buf[rows, re_cols] = n_re
                buf[rows, im_cols] = n_im
                state[2 * j], state[2 * j + 1] = n_re, n_im
        for p in range(n_piece):
            st_ref[:, p * SLAB_STATE:(p + 1) * SLAB_STATE] = state[p]

    def readout(buf):
        for j in range(N_SLABS):
            s_re = buf[:, j * width:j * width + SLAB_STATE].astype(BF16)
            s_im = buf[:, j * width + SLAB_STATE:(j + 1) * width].astype(BF16)
            y_ref[0, :, j * LANES:(j + 1) * LANES] = _dot(s_re, cre_ref[0, j]) - _dot(s_im, cim_ref[0, j])

    @pl.when(i == 0)
    def _():
        st_ref[...] = jnp.zeros_like(st_ref)
        input_map(ucur_ref, bufs[0])
        bufs[SCAN_BUFFERS - 1][...] = jnp.zeros_like(bufs[SCAN_BUFFERS - 1])

    for r in range(SCAN_BUFFERS):
        @pl.when(lax.rem(i, SCAN_BUFFERS) == r)
        def _(r=r):
            input_map(unext_ref, bufs[(r + 1) % SCAN_BUFFERS])
            recurrence(bufs[r])
            readout(bufs[(r + SCAN_BUFFERS - 1) % SCAN_BUFFERS])


def _s5_scan(u_tm, w_b, c_re, c_im, a_re, a_im):
    dir_map = lambda d, i: (d, 0, 0, 0)
    vec_map = lambda d, i: (d, 0, 0)
    last = SCAN_CHUNKS - 1
    return pl.pallas_call(
        _s5_scan_kernel,
        out_shape=jax.ShapeDtypeStruct((2, N_TOK * BATCH, SSM_WIDTH), F32),
        grid=(2, SCAN_CHUNKS + 1),
        in_specs=[
            pl.BlockSpec((SCAN_ROWS, SSM_WIDTH), lambda d, i: (_scan_chunk(d, jnp.minimum(i, last)), 0)),
            pl.BlockSpec((SCAN_ROWS, SSM_WIDTH), lambda d, i: (_scan_chunk(d, jnp.minimum(i + 1, last)), 0)),
            pl.BlockSpec((1, N_SLABS, LANES, 2 * SLAB_STATE), dir_map),
            pl.BlockSpec((1, N_SLABS, SLAB_STATE, LANES), dir_map),
            pl.BlockSpec((1, N_SLABS, SLAB_STATE, LANES), dir_map),
            pl.BlockSpec((1, BATCH, N_STATE), vec_map),
            pl.BlockSpec((1, BATCH, N_STATE), vec_map),
        ],
        out_specs=pl.BlockSpec((1, SCAN_ROWS, SSM_WIDTH),
                               lambda d, i: (d, _scan_chunk(d, jnp.maximum(i - 1, 0)), 0)),
        scratch_shapes=[pltpu.VMEM((SCAN_ROWS, 2 * N_STATE), F32) for _ in range(SCAN_BUFFERS)] + [
            pltpu.VMEM((BATCH, 2 * N_STATE), F32),
        ],
        compiler_params=_params("arbitrary", "arbitrary"),
        name="s5_scan",
    )(u_tm, u_tm, w_b, c_re, c_im, a_re, a_im)


HALO_BLOCKS = TIME_BLOCK // SUBLANES


def _even_out_kernel(y_ref, u_ref, p_ref, pprev_ref, pnext_ref, q_ref, zs_ref, ctx_ref, x_ref, mod_ref,
                     d_ref, gw_ref, gb_ref, cw_ref, wo_ref, perm_ref, o_ref):
    i = pl.program_id(0)
    flat = lambda ref: ref[...].reshape(EVEN_ROWS, ref.shape[-1])
    y_hi, y_lo = _split_bf16(y_ref[0] + y_ref[1])
    parts = []
    for part in range(PERM_PARTS):
        rows = slice(part * PERM_ROWS, (part + 1) * PERM_ROWS)
        y_part = _dot(perm_ref[...], y_hi[rows]) + _dot(perm_ref[...], y_lo[rows])
        parts.append(y_part.reshape(BATCH, PERM_STEPS, SSM_WIDTH))
    y = jnp.concatenate(parts, axis=1).reshape(EVEN_ROWS, SSM_WIDTH) + d_ref[...] * flat(u_ref)
    g = jax.nn.gelu(y)
    z = _dot(g.astype(BF16), gw_ref[...]) + gb_ref[...]
    ssm = g * jax.nn.sigmoid(z) * flat(zs_ref)

    p = flat(p_ref)
    t_in_block = lax.broadcasted_iota(jnp.int32, p.shape, 0) % TIME_BLOCK
    prev_ok = jnp.logical_and(i != 0, i != CTX_TIME_BLOCKS).astype(F32)
    next_ok = jnp.logical_and(i != CTX_TIME_BLOCKS - 1, i != TIME_BLOCKS - 1).astype(F32)
    halo_shape = (BATCH, TIME_BLOCK, CONV_WIDTH)
    halo_prev = jnp.broadcast_to(pprev_ref[:, SUBLANES - 1:SUBLANES, :] * prev_ok, halo_shape)
    halo_next = jnp.broadcast_to(pnext_ref[:, 0:1, :] * next_ok, halo_shape)
    p_prev = jnp.where(t_in_block == 0, halo_prev.reshape(p.shape), pltpu.roll(p, 1, 0))
    p_next = jnp.where(t_in_block == TIME_BLOCK - 1, halo_next.reshape(p.shape), pltpu.roll(p, EVEN_ROWS - 1, 0))
    cw = cw_ref[...]
    conv = flat(q_ref) * (p_prev * cw[0:1, :] + p * cw[1:2, :] + p_next * cw[2:3, :])

    out = _dot(conv.astype(BF16), wo_ref[0:CONV_WIDTH, :]) + _dot(ssm.astype(BF16), wo_ref[CONV_WIDTH:, :])
    gate = mod_ref[0][:, 2:3, :]
    o_ref[...] = _pick_hidden_time(ctx_ref, x_ref) + gate * out.reshape(BATCH, TIME_BLOCK, D_MODEL)


def _even_out(y_tm, u, p, q, zs, ctx, x, mod, d_skip, glu_w, glu_b, conv_w, w_out, perm_t):
    halo_prev = lambda i: (0, jnp.maximum(i * HALO_BLOCKS - 1, 0), 0)
    halo_next = lambda i: (0, jnp.minimum((i + 1) * HALO_BLOCKS, N_TOK // SUBLANES - 1), 0)
    tok_spec = pl.BlockSpec((BATCH, TIME_BLOCK, CONV_WIDTH), _tok_time_map)
    halo = lambda m: pl.BlockSpec((BATCH, SUBLANES, CONV_WIDTH), m)
    return pl.pallas_call(
        _even_out_kernel,
        out_shape=jax.ShapeDtypeStruct((BATCH, N_TOK, D_MODEL), F32),
        grid=(TIME_BLOCKS,),
        in_specs=[
            pl.BlockSpec((2, EVEN_ROWS, SSM_WIDTH), lambda i: (0, i, 0)),
            tok_spec, tok_spec, halo(halo_prev), halo(halo_next), tok_spec, tok_spec,
            pl.BlockSpec((BATCH, TIME_BLOCK, D_MODEL), _ctx_time_map),
            pl.BlockSpec((BATCH, TIME_BLOCK, D_MODEL), _lat_time_map),
            pl.BlockSpec((1, BATCH, 3, D_MODEL), _mod_time_map),
            pl.BlockSpec((1, SSM_WIDTH), _const1),
            pl.BlockSpec((SSM_WIDTH, SSM_WIDTH), _const1),
            pl.BlockSpec((1, SSM_WIDTH), _const1),
            pl.BlockSpec((3, CONV_WIDTH), _const1),
            pl.BlockSpec((CONV_WIDTH + SSM_WIDTH, D_MODEL), _const1),
            pl.BlockSpec((PERM_ROWS, PERM_ROWS), _const1),
        ],
        out_specs=pl.BlockSpec((BATCH, TIME_BLOCK, D_MODEL), _tok_time_map),
        compiler_params=_params("arbitrary"),
        name="even_out",
    )(y_tm, u, p, p, p, q, zs, ctx, x, mod, d_skip, glu_w, glu_b, conv_w, w_out, perm_t)


def _rope(v, cos_t, sin_lo, sin_hi):
    half = QK_ROPE // 2
    return v * cos_t + pltpu.roll(v, LANES - half, 1) * sin_lo + pltpu.roll(v, half, 1) * sin_hi


_NT = (((1,), (1,)), ((), ()))


def _mla_in_kernel(h_ref, mod_ref, nw_ref, win_ref, qan_ref, wuqt_ref, kvan_ref, wuk_ref, wuvt_ref,
                   gqt_ref, gkn_ref, gkr_ref, segk_ref, cos_ref, slo_ref, shi_ref, cost_ref, sint_ref,
                   q_ref, k_ref, vt_ref, gate_ref):
    a = _modulated_norm(h_ref[0], nw_ref[...], mod_ref[0])
    r = _dot(a.astype(BF16), win_ref[...])
    cq = r[:, :Q_LORA]
    ckv = r[:, Q_LORA:Q_LORA + KV_LORA]
    kr = r[:, Q_LORA + KV_LORA:Q_LORA + KV_LORA + LANES]
    gate = r[:, Q_LORA + KV_LORA + LANES:]
    gate_ref[0] = gate * jax.nn.sigmoid(gate)

    cqn = _rms(cq, qan_ref[...]).astype(BF16)
    ckvn = _rms(ckv, kvan_ref[...]).astype(BF16)
    qt = lax.dot_general(wuqt_ref[...], cqn, _NT, preferred_element_type=F32)
    kf = _dot(ckvn, wuk_ref[...])
    vt = lax.dot_general(wuvt_ref[...], ckvn, _NT, preferred_element_type=F32)
    ones = jnp.ones((MLA_HEADS, V_ROWS - V_HEAD, TOK_BLOCK), BF16)
    vt_ref[0, :, 0] = jnp.concatenate([vt.astype(BF16).reshape(MLA_HEADS, V_HEAD, TOK_BLOCK), ones], axis=1)

    half = QK_ROPE // 2
    cos_q, sin_q = cost_ref[...], sint_ref[...]
    g_nope, g_rope = gqt_ref[0:QK_NOPE, :], gqt_ref[QK_NOPE:QK_DIM, :]
    pad_rows = jnp.zeros((HEAD_PAD - QK_DIM, TOK_BLOCK), F32)
    for h in range(MLA_HEADS):
        qh = qt[h * HEAD_PAD:(h + 1) * HEAD_PAD, :]
        sq = qh * qh
        ms_nope = jnp.sum(sq[0:QK_NOPE], axis=0, keepdims=True) * (1.0 / QK_NOPE)
        ms_rope = jnp.sum(sq[QK_NOPE:QK_DIM], axis=0, keepdims=True) * (1.0 / QK_ROPE)
        nope = qh[0:QK_NOPE] * lax.rsqrt(ms_nope + EPS) * g_nope
        rope = qh[QK_NOPE:QK_DIM] * lax.rsqrt(ms_rope + EPS) * g_rope
        x1, x2 = rope[0:half], rope[half:]
        tile = jnp.concatenate([nope, x1 * cos_q - x2 * sin_q, x1 * sin_q + x2 * cos_q, pad_rows], axis=0)
        q_ref[0, h, 0] = (tile * SOFTMAX_SCALE).astype(BF16)

    cos_t, sin_lo, sin_hi = cos_ref[...], slo_ref[...], shi_ref[...]
    kr_ms = jnp.sum(kr * kr, axis=-1, keepdims=True) * (1.0 / QK_ROPE)
    k_rope = _rope(kr * lax.rsqrt(kr_ms + EPS) * gkr_ref[...], cos_t, sin_lo, sin_hi)
    for j in range(MLA_HEADS * HEAD_PAD // SEG_TILE):
        kj = kf[:, j * SEG_TILE:(j + 1) * SEG_TILE]
        k_ms = _dot((kj * kj).astype(BF16), segk_ref[...]) * (1.0 / QK_NOPE)
        kn = kj * lax.rsqrt(k_ms + EPS) * gkn_ref[...]
        for i in range(SEG_TILE // HEAD_PAD):
            k_ref[0, j * (SEG_TILE // HEAD_PAD) + i] = (kn[:, i * HEAD_PAD:(i + 1) * HEAD_PAD] + k_rope).astype(BF16)


def _mla_in(h, mod, norm_w, w_in, q_a_norm, w_uq_t, kv_a_norm, w_uk, w_uv_t, gq_t, gkn, gkr, seg_k,
            cos_t, sin_lo, sin_hi, cos_q, sin_q):
    head = jax.ShapeDtypeStruct((BATCH, MLA_HEADS, N_TOK, HEAD_PAD), BF16)
    head_spec = pl.BlockSpec((1, MLA_HEADS, TOK_BLOCK, HEAD_PAD), lambda b, t: (b, 0, t, 0))
    per_qb = Q_BLOCK // TOK_BLOCK
    q_lat = jax.ShapeDtypeStruct((BATCH, MLA_HEADS, SEQ // Q_BLOCK, HEAD_PAD, Q_BLOCK), BF16)

    def q_map(b, t):
        lat = jnp.maximum(t - CTX_BLOCKS, 0)
        return (b, 0, lat // per_qb, 0, lat % per_qb)
    q_spec = pl.BlockSpec((1, MLA_HEADS, 1, HEAD_PAD, TOK_BLOCK), q_map)
    vt = jax.ShapeDtypeStruct((BATCH, MLA_HEADS, N_TOK // KV_CHUNK, V_ROWS, KV_CHUNK), BF16)
    vt_spec = pl.BlockSpec((1, MLA_HEADS, 1, V_ROWS, KV_CHUNK), lambda b, t: (b, 0, t, 0, 0))
    tab_spec = pl.BlockSpec((TOK_BLOCK, HEAD_PAD), lambda b, t: (t, 0))
    tab_t_spec = pl.BlockSpec((QK_ROPE // 2, TOK_BLOCK), lambda b, t: (0, t))
    row = lambda n: pl.BlockSpec((1, n), _const2)
    full = lambda *shape: pl.BlockSpec(shape, _const2)
    return pl.pallas_call(
        _mla_in_kernel,
        out_shape=(q_lat, head, vt, jax.ShapeDtypeStruct((BATCH, N_TOK, MLA_MIX), F32)),
        grid=(BATCH, N_BLOCKS),
        in_specs=[
            pl.BlockSpec((1, TOK_BLOCK, D_MODEL), _tok_map),
            pl.BlockSpec((1, 3, D_MODEL), _mod_map),
            row(D_MODEL),
            full(D_MODEL, ODD_IN_PAD),
            row(Q_LORA),
            full(MLA_HEADS * HEAD_PAD, Q_LORA),
            row(KV_LORA),
            full(KV_LORA, MLA_HEADS * HEAD_PAD),
            full(MLA_MIX, KV_LORA),
            full(HEAD_PAD, TOK_BLOCK), row(SEG_TILE), row(HEAD_PAD),
            full(SEG_TILE, SEG_TILE),
            tab_spec, tab_spec, tab_spec, tab_t_spec, tab_t_spec,
        ],
        out_specs=(q_spec, head_spec, vt_spec, pl.BlockSpec((1, TOK_BLOCK, MLA_MIX), _tok_map)),
        compiler_params=_params("arbitrary", "arbitrary"),
        name="mla_in",
    )(h, mod, norm_w, w_in, q_a_norm, w_uq_t, kv_a_norm, w_uk, w_uv_t, gq_t, gkn, gkr, seg_k,
      cos_t, sin_lo, sin_hi, cos_q, sin_q)


def _attention_kernel(q_ref, k_ref, vt_ref, o_ref, s_ref, ot_ref):
    n_chunks = N_TOK // KV_CHUNK
    groups = KV_CHUNK // SUBLANES
    n_qb = SEQ // Q_BLOCK
    assert HEADS_PER_STEP % 2 == 0

    def q_rows(qb):
        return pl.ds(pl.multiple_of(qb * Q_BLOCK, Q_BLOCK), Q_BLOCK)

    def chunk(c):
        return slice(c * KV_CHUNK, (c + 1) * KV_CHUNK)

    def stages(score, value, mx_prev):
        mx = m_val = None
        if value is not None:
            m_val = jnp.max(mx_prev, axis=0, keepdims=True)
        for c in range(n_chunks):
            if score is not None:
                qb, hh = score
                sc = _dot(k_ref[0, hh, chunk(c), :], q_ref[0, hh, qb])
                s_ref[hh % 2, chunk(c), :] = sc
                cm = jnp.max(sc.reshape(groups, SUBLANES, Q_BLOCK), axis=0)
                mx = cm if mx is None else jnp.maximum(mx, cm)
            if value is not None:
                qb, hh = value
                e = jnp.exp2(s_ref[hh % 2, chunk(c), :] - m_val)
                pv = _dot(vt_ref[0, hh, c], e.astype(BF16))
                acc = pv if c == 0 else acc + pv
        if value is not None:
            qb, hh = value
            inv = 1.0 / acc[V_HEAD:V_HEAD + 1, :]
            ot_ref[hh * V_HEAD:(hh + 1) * V_HEAD, :] = acc[0:V_HEAD, :] * inv
            if hh == HEADS_PER_STEP - 1:
                o_ref[0, q_rows(qb), :] = ot_ref[...].T
        return mx

    def within_block(qb, mx):
        for hh in range(HEADS_PER_STEP - 1):
            mx = stages((qb, hh + 1), (qb, hh), mx)
        return mx

    def block(qb, mx):
        return stages((qb + 1, 0), (qb, HEADS_PER_STEP - 1), within_block(qb, mx))

    mx = stages((0, 0), None, None)
    mx = lax.fori_loop(0, n_qb - 1, block, mx)
    mx = within_block(n_qb - 1, mx)
    stages(None, (n_qb - 1, HEADS_PER_STEP - 1), mx)


def _attention(q, k, vt):
    pairs = MLA_HEADS // HEADS_PER_STEP
    pair_map = lambda b, hp: (b, hp, 0, 0)
    return pl.pallas_call(
        _attention_kernel,
        out_shape=jax.ShapeDtypeStruct((BATCH, SEQ, MLA_MIX), F32),
        grid=(BATCH, pairs),
        in_specs=[
            pl.BlockSpec((1, HEADS_PER_STEP, SEQ // Q_BLOCK, HEAD_PAD, Q_BLOCK), lambda b, hp: (b, hp, 0, 0, 0)),
            pl.BlockSpec((1, HEADS_PER_STEP, N_TOK, HEAD_PAD), pair_map),
            pl.BlockSpec((1, HEADS_PER_STEP, N_TOK // KV_CHUNK, V_ROWS, KV_CHUNK), lambda b, hp: (b, hp, 0, 0, 0)),
        ],
        out_specs=pl.BlockSpec((1, SEQ, HEADS_PER_STEP * V_HEAD), lambda b, hp: (b, 0, hp)),
        scratch_shapes=[
            pltpu.VMEM((2, N_TOK, Q_BLOCK), F32),
            pltpu.VMEM((HEADS_PER_STEP * V_HEAD, Q_BLOCK), F32),
        ],
        compiler_params=_params("arbitrary", "arbitrary"),
        name="attention",
    )(q, k, vt)


def _mla_out_kernel(att_ref, gate_ref, h_ref, mod_ref, wo_ref, o_ref):
    mix = (att_ref[0] * gate_ref[0]).astype(BF16)
    o_ref[0] = h_ref[0] + mod_ref[0, 2:3, :] * _dot(mix, wo_ref[...])


def _mla_out(att, gate, h, mod, w_out):
    lat_tok = lambda b, t: (b, t + CTX_BLOCKS, 0)
    return pl.pallas_call(
        _mla_out_kernel,
        out_shape=jax.ShapeDtypeStruct((BATCH, SEQ, D_MODEL), F32),
        grid=(BATCH, SEQ // TOK_BLOCK),
        in_specs=[
            pl.BlockSpec((1, TOK_BLOCK, MLA_MIX), _tok_map),
            pl.BlockSpec((1, TOK_BLOCK, MLA_MIX), lat_tok),
            pl.BlockSpec((1, TOK_BLOCK, D_MODEL), lat_tok),
            pl.BlockSpec((1, 3, D_MODEL), lambda b, t: (b, 0, 0)),
            pl.BlockSpec((MLA_MIX, D_MODEL), _const2),
        ],
        out_specs=pl.BlockSpec((1, TOK_BLOCK, D_MODEL), _tok_map),
        compiler_params=_params("arbitrary", "arbitrary"),
        name="mla_out",
    )(att, gate, h, mod, w_out)


def _pad_heads(w, width, offset=0):
    k = w.shape[0]
    w = w.reshape(k, MLA_HEADS, width)
    return jnp.pad(w, ((0, 0), (0, 0), (offset, HEAD_PAD - width - offset))).reshape(k, MLA_HEADS * HEAD_PAD)


def _mla_weights(w_in, w_uq, w_ukv, q_norm, k_norm):
    cuts = [Q_LORA, Q_LORA + KV_LORA, Q_LORA + KV_LORA + QK_ROPE]
    w_kr = jnp.pad(w_in[:, cuts[1]:cuts[2]], ((0, 0), (QK_NOPE, LANES - QK_DIM)))
    w_in_p = jnp.concatenate([w_in[:, :cuts[1]], w_kr, w_in[:, cuts[2]:]], axis=1).astype(BF16)
    w_uq_t = _pad_heads(w_uq, QK_DIM).T.astype(BF16)
    w_kv = w_ukv.reshape(KV_LORA, MLA_HEADS, QK_NOPE + V_HEAD)
    w_uk = _pad_heads(w_kv[:, :, :QK_NOPE].reshape(KV_LORA, -1), QK_NOPE).astype(BF16)
    w_uv_t = w_kv[:, :, QK_NOPE:].reshape(KV_LORA, MLA_MIX).T.astype(BF16)
    pad_row = lambda v, off: jnp.pad(v, (off, HEAD_PAD - v.shape[0] - off)).reshape(1, HEAD_PAD)
    gq_t = jnp.broadcast_to(pad_row(q_norm, 0).reshape(HEAD_PAD, 1), (HEAD_PAD, TOK_BLOCK))
    gkn = jnp.tile(pad_row(k_norm[:QK_NOPE], 0), (1, SEG_TILE // HEAD_PAD))
    gkr = pad_row(k_norm[QK_NOPE:], QK_NOPE)
    return w_in_p, w_uq_t, w_uk, w_uv_t, gq_t, gkn, gkr


def _key_segment_matrix():
    lane = np.arange(SEG_TILE)
    is_nope = lane % HEAD_PAD < QK_NOPE
    same_head = (lane // HEAD_PAD)[:, None] == (lane // HEAD_PAD)[None, :]
    return jnp.asarray(same_head & is_nope[:, None] & is_nope[None, :], dtype=BF16)


def _rope_tables():
    f32 = np.float32
    rows = SEQ // GRID_W
    row = np.repeat(np.arange(rows, dtype=f32), GRID_W)
    col = np.tile(np.arange(GRID_W, dtype=f32), rows)
    n_freq = QK_ROPE // 4
    inv = np.power(f32(ROPE_BASE), -np.arange(n_freq, dtype=f32) / f32(n_freq)).astype(f32)
    ang = np.concatenate([row[:, None] * inv, col[:, None] * inv], axis=-1).astype(f32)
    cos, sin = np.cos(ang).astype(f32), np.sin(ang).astype(f32)
    half = QK_ROPE // 2
    zeros = np.zeros_like(cos)
    pad_l = np.ones((SEQ, QK_NOPE), f32)
    tail = np.zeros((SEQ, HEAD_PAD - QK_DIM), f32)
    cos_t = np.concatenate([pad_l, cos, cos, tail], axis=-1)
    sin_lo = np.concatenate([0 * pad_l, -sin, zeros, tail], axis=-1)
    sin_hi = np.concatenate([0 * pad_l, zeros, sin, tail], axis=-1)
    ctx_cos = np.concatenate([np.ones((CTX_LEN, QK_DIM), f32), np.zeros((CTX_LEN, HEAD_PAD - QK_DIM), f32)], -1)
    ctx_zero = np.zeros((CTX_LEN, HEAD_PAD), f32)
    cos_q = np.concatenate([np.ones((half, CTX_LEN), f32), cos.T], axis=1)
    sin_q = np.concatenate([np.zeros((half, CTX_LEN), f32), sin.T], axis=1)
    tables = (np.concatenate([ctx_cos, cos_t], 0), np.concatenate([ctx_zero, sin_lo], 0),
              np.concatenate([ctx_zero, sin_hi], 0), cos_q, sin_q)
    return tuple(jnp.asarray(t) for t in tables)


def kernel(x, c, ctx, c_ctx, mod_w, mod_b, norm_w, e_w_in, e_conv_w, e_lam_re, e_lam_im, e_log_step, e_b_re, e_b_im, e_c_re, e_c_im, e_d, e_glu_w, e_glu_b, e_w_out, o_w_in, o_q_a_norm, o_w_uq, o_kv_a_norm, o_w_ukv, o_q_norm, o_k_norm, o_w_out):
    cond = jnp.zeros((COND_ROWS, D_MODEL), F32).at[:BATCH].set(c).at[CTX_ROW].set(c_ctx)
    mod = _modulation(cond, mod_w, mod_b)[:, :CTX_ROW + 1].reshape(2, CTX_ROW + 1, 3, D_MODEL)

    mod_even = jnp.stack([mod[0, :BATCH], jnp.broadcast_to(mod[0, CTX_ROW], (BATCH, 3, D_MODEL))])
    perm, perm_t = _time_major_perm()
    p, q, zs, u, u_tm = _even_in(ctx, x, mod_even, norm_w[0:1], e_w_in[0].astype(BF16), perm)
    a_re, a_im, bbr, bbi = _s5_discretize(e_lam_re[0], e_lam_im[0], e_log_step[0], e_b_re[0], e_b_im[0])
    w_b, c_re, c_im = _s5_matrices(bbr, bbi, e_c_re[0], e_c_im[0])
    bcast = lambda a: jnp.broadcast_to(a[:, None, :], (2, BATCH, N_STATE))
    y = _s5_scan(u_tm, w_b, c_re, c_im, bcast(a_re), bcast(a_im))
    h1 = _even_out(y, u, p, q, zs, ctx, x, mod_even, e_d[0:1], e_glu_w[0].astype(BF16), e_glu_b[0:1],
                   e_conv_w[0], e_w_out[0].astype(BF16), perm_t)

    w_in_p, w_uq_t, w_uk, w_uv_t, gq_t, gkn, gkr = _mla_weights(o_w_in[0], o_w_uq[0], o_w_ukv[0],
                                                                o_q_norm[0], o_k_norm[0])
    cos_t, sin_lo, sin_hi, cos_q, sin_q = _rope_tables()
    qh, kh, vh, gate = _mla_in(h1, mod[1], norm_w[1:2], w_in_p, o_q_a_norm[0:1], w_uq_t, o_kv_a_norm[0:1],
                               w_uk, w_uv_t, gq_t, gkn, gkr, _key_segment_matrix(),
                               cos_t, sin_lo, sin_hi, cos_q, sin_q)
    att = _attention(qh, kh, vh)
    return _mla_out(att, gate, h1, mod[1], o_w_out[0].astype(BF16))
```

```python
import functools
import math

import jax
import jax.numpy as jnp
import numpy as np
from jax import lax
from jax.experimental import pallas as pl
from jax.experimental.pallas import tpu as pltpu

F32 = jnp.float32
BF16 = jnp.bfloat16

D_MODEL = 1024
BATCH = 8
SEQ = 4096
CTX_LEN = 256
GRID_W = 64
EPS = 1e-6
CONV_WIDTH = 512
SSM_WIDTH = 512
SSM_GROUP = 16
SSM_GROUPS = SSM_WIDTH // SSM_GROUP
SSM_STATE = 64
MLA_HEADS = 16
QK_NOPE = 64
QK_ROPE = 32
QK_DIM = QK_NOPE + QK_ROPE
V_HEAD = 64
Q_LORA = 384
KV_LORA = 256
MLA_MIX = MLA_HEADS * V_HEAD
ROPE_BASE = 10000.0

LANES = 128
SUBLANES = 8
VMEM_BYTES_V7X = 64 * 1024 * 1024
VMEM_LIMIT_BYTES = VMEM_BYTES_V7X * 7 // 8

N_TOK = CTX_LEN + SEQ
TOK_BLOCK = 256
N_BLOCKS = N_TOK // TOK_BLOCK
CTX_BLOCKS = CTX_LEN // TOK_BLOCK
assert CTX_BLOCKS == 1 and N_TOK % TOK_BLOCK == 0
COND_ROWS = 2 * SUBLANES
CTX_ROW = BATCH
EVEN_IN = 4 * CONV_WIDTH + 2 * SSM_WIDTH

SCAN_STEPS = 64
SCAN_ROWS = SCAN_STEPS * BATCH
SCAN_CHUNKS = N_TOK // SCAN_STEPS
SCAN_CTX_CHUNKS = CTX_LEN // SCAN_STEPS
SLAB_GROUPS = LANES // SSM_GROUP
N_SLABS = SSM_GROUPS // SLAB_GROUPS
SLAB_STATE = SLAB_GROUPS * SSM_STATE
N_STATE = SSM_GROUPS * SSM_STATE

HEAD_PAD = LANES
SEG_TILE = 2 * LANES
Q_BLOCK = 512
HEADS_PER_STEP = 4
KV_CHUNK = 256
V_ROWS = V_HEAD + 2 * SUBLANES
assert KV_CHUNK == TOK_BLOCK
ODD_IN_PAD = Q_LORA + KV_LORA + LANES + MLA_MIX
SOFTMAX_SCALE = QK_DIM ** -0.5 * math.log2(math.e)


def _params(*sem):
    return pltpu.CompilerParams(dimension_semantics=sem, vmem_limit_bytes=VMEM_LIMIT_BYTES)


def _dot(a, b):
    return jnp.dot(a, b, preferred_element_type=F32)


def _split_bf16(a):
    hi = a.astype(BF16)
    lo = (a - hi.astype(F32)).astype(BF16)
    return hi, lo


def _dot3(a, b):
    ah, al = _split_bf16(a)
    bh, bl = _split_bf16(b)
    return _dot(ah, bh) + (_dot(ah, bl) + _dot(al, bh))


def _rms(x, w):
    return x * lax.rsqrt(jnp.mean(x * x, axis=-1, keepdims=True) + EPS) * w


def _modulated_norm(h, norm_w, mod):
    return _rms(h, norm_w) * (1.0 + mod[1:2, :]) + mod[0:1, :]


def _mod_kernel(cond_ref, w_ref, b_ref, o_ref):
    cond = cond_ref[...]
    o_ref[0] = _dot3(cond * jax.nn.sigmoid(cond), w_ref[0]) + b_ref[0]


def _modulation(cond, mod_w, mod_b):
    depth = mod_w.shape[0]
    n_col = 3
    return pl.pallas_call(
        _mod_kernel,
        out_shape=jax.ShapeDtypeStruct((depth, COND_ROWS, 3 * D_MODEL), F32),
        grid=(depth, n_col),
        in_specs=[
            pl.BlockSpec((COND_ROWS, D_MODEL), lambda i, j: (0, 0)),
            pl.BlockSpec((1, D_MODEL, D_MODEL), lambda i, j: (i, 0, j)),
            pl.BlockSpec((1, 1, D_MODEL), lambda i, j: (i, 0, j)),
        ],
        out_specs=pl.BlockSpec((1, COND_ROWS, D_MODEL), lambda i, j: (i, 0, j)),
        compiler_params=_params("arbitrary", "arbitrary"),
        name="modulation",
    )(cond, mod_w, mod_b.reshape(depth, 1, 3 * D_MODEL))


def _tok_map(b, t):
    return (b, t, 0)


def _mod_map(b, t):
    return (jnp.where(t < CTX_BLOCKS, CTX_ROW, b), 0, 0)


def _const2(b, t):
    return (0, 0)


EVEN_ROWS = 512
TIME_BLOCK = EVEN_ROWS // BATCH
TIME_BLOCKS = N_TOK // TIME_BLOCK
CTX_TIME_BLOCKS = CTX_LEN // TIME_BLOCK
assert N_TOK % TIME_BLOCK == 0 and CTX_LEN % TIME_BLOCK == 0


PERM_ROWS = 256
PERM_STEPS = PERM_ROWS // BATCH
PERM_PARTS = EVEN_ROWS // PERM_ROWS


def _time_major_perm():
    r = np.arange(PERM_ROWS)
    src = (r % BATCH) * PERM_STEPS + r // BATCH
    perm = src[:, None] == np.arange(PERM_ROWS)[None, :]
    return jnp.asarray(perm, dtype=BF16), jnp.asarray(perm.T, dtype=BF16)


def _time_part(v, part):
    v3 = v.reshape(BATCH, TIME_BLOCK, v.shape[-1])
    return v3[:, part * PERM_STEPS:(part + 1) * PERM_STEPS, :].reshape(PERM_ROWS, v.shape[-1])


def _ctx_time_map(i):
    return (0, jnp.minimum(i, CTX_TIME_BLOCKS - 1), 0)


def _lat_time_map(i):
    return (0, jnp.maximum(i - CTX_TIME_BLOCKS, 0), 0)


def _tok_time_map(i):
    return (0, i, 0)


def _mod_time_map(i):
    return (jnp.where(i < CTX_TIME_BLOCKS, 1, 0), 0, 0, 0)


def _const1(i):
    return (0, 0)


def _pick_hidden_time(ctx_ref, x_ref):
    return jnp.where(pl.program_id(0) < CTX_TIME_BLOCKS, ctx_ref[...], x_ref[...])


def _even_in_kernel(ctx_ref, x_ref, mod_ref, nw_ref, w_ref, perm_ref, p_ref, q_ref, zs_ref, u_ref, utm_ref):
    h = _pick_hidden_time(ctx_ref, x_ref)
    mod = mod_ref[0]
    a = _rms(h, nw_ref[...]) * (1.0 + mod[:, 1:2, :]) + mod[:, 0:1, :]
    r = _dot(a.reshape(EVEN_ROWS, D_MODEL).astype(BF16), w_ref[...])
    cw = CONV_WIDTH
    xa, ba, ca, za = (r[:, i * cw:(i + 1) * cw] for i in range(4))
    us = r[:, 4 * cw:4 * cw + SSM_WIDTH]
    zs = r[:, 4 * cw + SSM_WIDTH:]
    blk = (BATCH, TIME_BLOCK, CONV_WIDTH)
    p_ref[...] = (ca * xa).reshape(blk)
    q_ref[...] = (ba * (za * jax.nn.sigmoid(za))).reshape(blk)
    zs_ref[...] = (zs * jax.nn.sigmoid(zs)).reshape(blk)
    u_ref[...] = us.reshape(blk)
    ub = us.astype(BF16)
    for part in range(PERM_PARTS):
        utm_ref[part * PERM_ROWS:(part + 1) * PERM_ROWS, :] = _dot(perm_ref[...], _time_part(ub, part)).astype(BF16)


def _even_in(ctx, x, mod, norm_w, w_in, perm):
    tok = jax.ShapeDtypeStruct((BATCH, N_TOK, CONV_WIDTH), F32)
    tok_spec = pl.BlockSpec((BATCH, TIME_BLOCK, CONV_WIDTH), _tok_time_map)
    return pl.pallas_call(
        _even_in_kernel,
        out_shape=(tok, tok, tok, tok, jax.ShapeDtypeStruct((N_TOK * BATCH, SSM_WIDTH), BF16)),
        grid=(TIME_BLOCKS,),
        in_specs=[
            pl.BlockSpec((BATCH, TIME_BLOCK, D_MODEL), _ctx_time_map),
            pl.BlockSpec((BATCH, TIME_BLOCK, D_MODEL), _lat_time_map),
            pl.BlockSpec((1, BATCH, 3, D_MODEL), _mod_time_map),
            pl.BlockSpec((1, D_MODEL), _const1),
            pl.BlockSpec((D_MODEL, EVEN_IN), _const1),
            pl.BlockSpec((PERM_ROWS, PERM_ROWS), _const1),
        ],
        out_specs=(tok_spec, tok_spec, tok_spec, tok_spec,
                   pl.BlockSpec((EVEN_ROWS, SSM_WIDTH), lambda i: (i, 0))),
        compiler_params=_params("arbitrary"),
        name="even_in",
    )(ctx, x, mod, norm_w, w_in, perm)


def _s5_disc_kernel(lr_ref, li_ref, ls_ref, br_ref, bi_ref, ar_ref, ai_ref, bbr_ref, bbi_ref):
    lr = lr_ref[...]
    li = li_ref[...]
    dt = jnp.exp(ls_ref[...])
    mag = jnp.exp(lr * dt)
    ar = mag * jnp.cos(li * dt)
    ai = mag * jnp.sin(li * dt)
    nr = ar - 1.0
    den = lr * lr + li * li
    fr = (nr * lr + ai * li) / den
    fi = (ai * lr - nr * li) / den
    ar_ref[...] = ar
    ai_ref[...] = ai
    br = br_ref[...]
    bi = bi_ref[...]
    bbr_ref[...] = fr[:, None, :] * br - fi[:, None, :] * bi
    bbi_ref[...] = fr[:, None, :] * bi + fi[:, None, :] * br


def _s5_discretize(lam_re, lam_im, log_step, b_re, b_im):
    flat = lambda a: a.reshape(2, N_STATE)
    chan = lambda a: jnp.transpose(a, (0, 3, 1, 2)).reshape(2, SSM_GROUP, N_STATE)
    ls = jnp.repeat(log_step, SSM_STATE, axis=-1)
    vec = jax.ShapeDtypeStruct((2, N_STATE), F32)
    mat = jax.ShapeDtypeStruct((2, SSM_GROUP, N_STATE), F32)
    return pl.pallas_call(_s5_disc_kernel, out_shape=(vec, vec, mat, mat), name="s5_discretize")(
        flat(lam_re), flat(lam_im), ls, chan(b_re), chan(b_im))


def _block_diag_groups(a):
    d, j, g, r, c = a.shape
    eye = jnp.eye(g, dtype=bool)[None, None, :, None, :, None]
    full = jnp.where(eye, a[:, :, :, :, None, :], jnp.zeros((), a.dtype))
    return full.reshape(d, j, g * r, g * c)


def _s5_matrices(bbr, bbi, c_re, c_im):
    def in_map(bb):
        a = bb.reshape(2, SSM_GROUP, N_SLABS, SLAB_GROUPS, SSM_STATE)
        return _block_diag_groups(jnp.transpose(a, (0, 2, 3, 1, 4)))
    w_b = jnp.concatenate([in_map(bbr), in_map(bbi)], axis=-1).astype(BF16)

    def out_map(cc):
        a = cc.reshape(2, N_SLABS, SLAB_GROUPS, SSM_GROUP, SSM_STATE)
        return _block_diag_groups(jnp.transpose(a, (0, 1, 2, 4, 3))).astype(BF16)
    return w_b, out_map(c_re), out_map(c_im)


def _scan_chunk(d, i):
    back = jnp.where(i < SCAN_CTX_CHUNKS, SCAN_CTX_CHUNKS - 1 - i,
                     SCAN_CHUNKS - 1 - (i - SCAN_CTX_CHUNKS))
    return jnp.where(d == 0, i, back)


SCAN_BUFFERS = 3


def _s5_scan_kernel(backward, ucur_ref, unext_ref, wb_ref, cre_ref, cim_ref, ar_ref, ai_ref, *rest):
    yprev_ref = rest[0] if backward else None
    y_ref = rest[1] if backward else rest[0]
    scratch = rest[2:] if backward else rest[1:]
    bufs, st_ref = scratch[:SCAN_BUFFERS], scratch[SCAN_BUFFERS]
    i = pl.program_id(0)
    width = 2 * SLAB_STATE
    n_piece = 2 * N_SLABS

    def input_map(u_ref, buf):
        ub = u_ref[...]
        for j in range(N_SLABS):
            buf[:, j * width:(j + 1) * width] = _dot(ub[:, j * LANES:(j + 1) * LANES], wb_ref[0, j])

    def recurrence(buf):
        state = [st_ref[:, p * SLAB_STATE:(p + 1) * SLAB_STATE] for p in range(n_piece)]
        for k in range(SCAN_STEPS):
            t = SCAN_STEPS - 1 - k if backward else k
            rows = slice(t * BATCH, (t + 1) * BATCH)
            for j in range(N_SLABS):
                s_re, s_im = state[2 * j], state[2 * j + 1]
                a_re = ar_ref[0, :, j * SLAB_STATE:(j + 1) * SLAB_STATE]
                a_im = ai_ref[0, :, j * SLAB_STATE:(j + 1) * SLAB_STATE]
                re_cols = slice(j * width, j * width + SLAB_STATE)
                im_cols = slice(j * width + SLAB_STATE, (j + 1) * width)
                n_re = a_re * s_re - a_im * s_im + buf[rows, re_cols]
                n_im = a_re * s_im + a_im * s_re + buf[rows, im_cols]
                buf[rows, re_cols] = n_re
                buf[rows, im_cols] = n_im
                state[2 * j], state[2 * j + 1] = n_re, n_im
        for p in range(n_piece):
            st_ref[:, p * SLAB_STATE:(p + 1) * SLAB_STATE] = state[p]

    def readout(buf):
        for j in range(N_SLABS):
            s_re = buf[:, j * width:j * width + SLAB_STATE].astype(BF16)
            s_im = buf[:, j * width + SLAB_STATE:(j + 1) * width].astype(BF16)
            cols = slice(j * LANES, (j + 1) * LANES)
            y = _dot(s_re, cre_ref[0, j]) - _dot(s_im, cim_ref[0, j])
            y_ref[:, cols] = yprev_ref[:, cols] + y if backward else y

    @pl.when(i == 0)
    def _():
        st_ref[...] = jnp.zeros_like(st_ref)
        input_map(ucur_ref, bufs[0])
        bufs[SCAN_BUFFERS - 1][...] = jnp.zeros_like(bufs[SCAN_BUFFERS - 1])

    for r in range(SCAN_BUFFERS):
        @pl.when(lax.rem(i, SCAN_BUFFERS) == r)
        def _(r=r):
            input_map(unext_ref, bufs[(r + 1) % SCAN_BUFFERS])
            recurrence(bufs[r])
            readout(bufs[(r + SCAN_BUFFERS - 1) % SCAN_BUFFERS])


def _s5_scan(direction, u_tm, w_b, c_re, c_im, a_re, a_im, y_prev=None):
    backward = direction == 1
    dir_map = lambda i: (direction, 0, 0, 0)
    vec_map = lambda i: (direction, 0, 0)
    last = SCAN_CHUNKS - 1
    chunk = lambda i: _scan_chunk(direction, jnp.clip(i, 0, last))
    y_spec = pl.BlockSpec((SCAN_ROWS, SSM_WIDTH), lambda i: (chunk(i - 1), 0))
    return pl.pallas_call(
        functools.partial(_s5_scan_kernel, backward),
        out_shape=jax.ShapeDtypeStruct((N_TOK * BATCH, SSM_WIDTH), F32),
        grid=(SCAN_CHUNKS + 1,),
        in_specs=[
            pl.BlockSpec((SCAN_ROWS, SSM_WIDTH), lambda i: (chunk(i), 0)),
            pl.BlockSpec((SCAN_ROWS, SSM_WIDTH), lambda i: (chunk(i + 1), 0)),
            pl.BlockSpec((1, N_SLABS, LANES, 2 * SLAB_STATE), dir_map),
            pl.BlockSpec((1, N_SLABS, SLAB_STATE, LANES), dir_map),
            pl.BlockSpec((1, N_SLABS, SLAB_STATE, LANES), dir_map),
            pl.BlockSpec((1, BATCH, N_STATE), vec_map),
            pl.BlockSpec((1, BATCH, N_STATE), vec_map),
        ] + ([y_spec] if backward else []),
        out_specs=y_spec,
        scratch_shapes=[pltpu.VMEM((SCAN_ROWS, 2 * N_STATE), F32) for _ in range(SCAN_BUFFERS)] + [
            pltpu.VMEM((BATCH, 2 * N_STATE), F32),
        ],
        input_output_aliases={7: 0} if backward else {},
        compiler_params=_params("arbitrary"),
        name="s5_scan_bwd" if backward else "s5_scan_fwd",
    )(u_tm, u_tm, w_b, c_re, c_im, a_re, a_im, *([y_prev] if backward else []))


HALO_BLOCKS = TIME_BLOCK // SUBLANES


def _even_out_kernel(y_ref, u_ref, p_ref, pprev_ref, pnext_ref, q_ref, zs_ref, ctx_ref, x_ref, mod_ref,
                     d_ref, gw_ref, gb_ref, cw_ref, wo_ref, perm_ref, o_ref):
    i = pl.program_id(0)
    flat = lambda ref: ref[...].reshape(EVEN_ROWS, ref.shape[-1])
    y_hi, y_lo = _split_bf16(y_ref[...])
    parts = []
    for part in range(PERM_PARTS):
        rows = slice(part * PERM_ROWS, (part + 1) * PERM_ROWS)
        y_part = _dot(perm_ref[...], y_hi[rows]) + _dot(perm_ref[...], y_lo[rows])
        parts.append(y_part.reshape(BATCH, PERM_STEPS, SSM_WIDTH))
    y = jnp.concatenate(parts, axis=1).reshape(EVEN_ROWS, SSM_WIDTH) + d_ref[...] * flat(u_ref)
    g = jax.nn.gelu(y)
    z = _dot(g.astype(BF16), gw_ref[...]) + gb_ref[...]
    ssm = g * jax.nn.sigmoid(z) * flat(zs_ref)

    p = flat(p_ref)
    t_in_block = lax.broadcasted_iota(jnp.int32, p.shape, 0) % TIME_BLOCK
    prev_ok = jnp.logical_and(i != 0, i != CTX_TIME_BLOCKS).astype(F32)
    next_ok = jnp.logical_and(i != CTX_TIME_BLOCKS - 1, i != TIME_BLOCKS - 1).astype(F32)
    halo_shape = (BATCH, TIME_BLOCK, CONV_WIDTH)
    halo_prev = jnp.broadcast_to(pprev_ref[:, SUBLANES - 1:SUBLANES, :] * prev_ok, halo_shape)
    halo_next = jnp.broadcast_to(pnext_ref[:, 0:1, :] * next_ok, halo_shape)
    p_prev = jnp.where(t_in_block == 0, halo_prev.reshape(p.shape), pltpu.roll(p, 1, 0))
    p_next = jnp.where(t_in_block == TIME_BLOCK - 1, halo_next.reshape(p.shape), pltpu.roll(p, EVEN_ROWS - 1, 0))
    cw = cw_ref[...]
    conv = flat(q_ref) * (p_prev * cw[0:1, :] + p * cw[1:2, :] + p_next * cw[2:3, :])

    out = _dot(conv.astype(BF16), wo_ref[0:CONV_WIDTH, :]) + _dot(ssm.astype(BF16), wo_ref[CONV_WIDTH:, :])
    gate = mod_ref[0][:, 2:3, :]
    o_ref[...] = _pick_hidden_time(ctx_ref, x_ref) + gate * out.reshape(BATCH, TIME_BLOCK, D_MODEL)


def _even_out(y_tm, u, p, q, zs, ctx, x, mod, d_skip, glu_w, glu_b, conv_w, w_out, perm_t):
    halo_prev = lambda i: (0, jnp.maximum(i * HALO_BLOCKS - 1, 0), 0)
    halo_next = lambda i: (0, jnp.minimum((i + 1) * HALO_BLOCKS, N_TOK // SUBLANES - 1), 0)
    tok_spec = pl.BlockSpec((BATCH, TIME_BLOCK, CONV_WIDTH), _tok_time_map)
    halo = lambda m: pl.BlockSpec((BATCH, SUBLANES, CONV_WIDTH), m)
    return pl.pallas_call(
        _even_out_kernel,
        out_shape=jax.ShapeDtypeStruct((BATCH, N_TOK, D_MODEL), F32),
        grid=(TIME_BLOCKS,),
        in_specs=[
            pl.BlockSpec((EVEN_ROWS, SSM_WIDTH), lambda i: (i, 0)),
            tok_spec, tok_spec, halo(halo_prev), halo(halo_next), tok_spec, tok_spec,
            pl.BlockSpec((BATCH, TIME_BLOCK, D_MODEL), _ctx_time_map),
            pl.BlockSpec((BATCH, TIME_BLOCK, D_MODEL), _lat_time_map),
            pl.BlockSpec((1, BATCH, 3, D_MODEL), _mod_time_map),
            pl.BlockSpec((1, SSM_WIDTH), _const1),
            pl.BlockSpec((SSM_WIDTH, SSM_WIDTH), _const1),
            pl.BlockSpec((1, SSM_WIDTH), _const1),
            pl.BlockSpec((3, CONV_WIDTH), _const1),
            pl.BlockSpec((CONV_WIDTH + SSM_WIDTH, D_MODEL), _const1),
            pl.BlockSpec((PERM_ROWS, PERM_ROWS), _const1),
        ],
        out_specs=pl.BlockSpec((BATCH, TIME_BLOCK, D_MODEL), _tok_time_map),
        compiler_params=_params("arbitrary"),
        name="even_out",
    )(y_tm, u, p, p, p, q, zs, ctx, x, mod, d_skip, glu_w, glu_b, conv_w, w_out, perm_t)


def _rope(v, cos_t, sin_lo, sin_hi):
    half = QK_ROPE // 2
    return v * cos_t + pltpu.roll(v, LANES - half, 1) * sin_lo + pltpu.roll(v, half, 1) * sin_hi


_NT = (((1,), (1,)), ((), ()))


def _mla_in_kernel(h_ref, mod_ref, nw_ref, win_ref, qan_ref, wuqt_ref, kvan_ref, wuk_ref, wuvt_ref,
                   gqt_ref, gkn_ref, gkr_ref, segk_ref, cos_ref, slo_ref, shi_ref, cost_ref, sint_ref,
                   q_ref, k_ref, vt_ref, gate_ref):
    a = _modulated_norm(h_ref[0], nw_ref[...], mod_ref[0])
    r = _dot(a.astype(BF16), win_ref[...])
    cq = r[:, :Q_LORA]
    ckv = r[:, Q_LORA:Q_LORA + KV_LORA]
    kr = r[:, Q_LORA + KV_LORA:Q_LORA + KV_LORA + LANES]
    gate = r[:, Q_LORA + KV_LORA + LANES:]
    gate_ref[0] = gate * jax.nn.sigmoid(gate)

    cqn = _rms(cq, qan_ref[...]).astype(BF16)
    ckvn = _rms(ckv, kvan_ref[...]).astype(BF16)
    qt = lax.dot_general(wuqt_ref[...], cqn, _NT, preferred_element_type=F32)
    kf = _dot(ckvn, wuk_ref[...])
    vt = lax.dot_general(wuvt_ref[...], ckvn, _NT, preferred_element_type=F32)
    ones = jnp.ones((MLA_HEADS, V_ROWS - V_HEAD, TOK_BLOCK), BF16)
    vt_ref[0, :, 0] = jnp.concatenate([vt.astype(BF16).reshape(MLA_HEADS, V_HEAD, TOK_BLOCK), ones], axis=1)

    half = QK_ROPE // 2
    cos_q, sin_q = cost_ref[...], sint_ref[...]
    g_nope, g_rope = gqt_ref[0:QK_NOPE, :], gqt_ref[QK_NOPE:QK_DIM, :]
    pad_rows = jnp.zeros((HEAD_PAD - QK_DIM, TOK_BLOCK), F32)
    for h in range(MLA_HEADS):
        qh = qt[h * HEAD_PAD:(h + 1) * HEAD_PAD, :]
        sq = qh * qh
        ms_nope = jnp.sum(sq[0:QK_NOPE], axis=0, keepdims=True) * (1.0 / QK_NOPE)
        ms_rope = jnp.sum(sq[QK_NOPE:QK_DIM], axis=0, keepdims=True) * (1.0 / QK_ROPE)
        nope = qh[0:QK_NOPE] * lax.rsqrt(ms_nope + EPS) * g_nope
        rope = qh[QK_NOPE:QK_DIM] * lax.rsqrt(ms_rope + EPS) * g_rope
        x1, x2 = rope[0:half], rope[half:]
        tile = jnp.concatenate([nope, x1 * cos_q - x2 * sin_q, x1 * sin_q + x2 * cos_q, pad_rows], axis=0)
        q_ref[0, h, 0] = (tile * SOFTMAX_SCALE).astype(BF16)

    cos_t, sin_lo, sin_hi = cos_ref[...], slo_ref[...], shi_ref[...]
    kr_ms = jnp.sum(kr * kr, axis=-1, keepdims=True) * (1.0 / QK_ROPE)
    k_rope = _rope(kr * lax.rsqrt(kr_ms + EPS) * gkr_ref[...], cos_t, sin_lo, sin_hi)
    for j in range(MLA_HEADS * HEAD_PAD // SEG_TILE):
        kj = kf[:, j * SEG_TILE:(j + 1) * SEG_TILE]
        k_ms = _dot((kj * kj).astype(BF16), segk_ref[...]) * (1.0 / QK_NOPE)
        kn = kj * lax.rsqrt(k_ms + EPS) * gkn_ref[...]
        for i in range(SEG_TILE // HEAD_PAD):
            k_ref[0, j * (SEG_TILE // HEAD_PAD) + i] = (kn[:, i * HEAD_PAD:(i + 1) * HEAD_PAD] + k_rope).astype(BF16)


def _mla_in(h, mod, norm_w, w_in, q_a_norm, w_uq_t, kv_a_norm, w_uk, w_uv_t, gq_t, gkn, gkr, seg_k,
            cos_t, sin_lo, sin_hi, cos_q, sin_q):
    head = jax.ShapeDtypeStruct((BATCH, MLA_HEADS, N_TOK, HEAD_PAD), BF16)
    head_spec = pl.BlockSpec((1, MLA_HEADS, TOK_BLOCK, HEAD_PAD), lambda b, t: (b, 0, t, 0))
    per_qb = Q_BLOCK // TOK_BLOCK
    q_lat = jax.ShapeDtypeStruct((BATCH, MLA_HEADS, SEQ // Q_BLOCK, HEAD_PAD, Q_BLOCK), BF16)

    def q_map(b, t):
        lat = jnp.maximum(t - CTX_BLOCKS, 0)
        return (b, 0, lat // per_qb, 0, lat % per_qb)
    q_spec = pl.BlockSpec((1, MLA_HEADS, 1, HEAD_PAD, TOK_BLOCK), q_map)
    vt = jax.ShapeDtypeStruct((BATCH, MLA_HEADS, N_TOK // KV_CHUNK, V_ROWS, KV_CHUNK), BF16)
    vt_spec = pl.BlockSpec((1, MLA_HEADS, 1, V_ROWS, KV_CHUNK), lambda b, t: (b, 0, t, 0, 0))
    tab_spec = pl.BlockSpec((TOK_BLOCK, HEAD_PAD), lambda b, t: (t, 0))
    tab_t_spec = pl.BlockSpec((QK_ROPE // 2, TOK_BLOCK), lambda b, t: (0, t))
    row = lambda n: pl.BlockSpec((1, n), _const2)
    full = lambda *shape: pl.BlockSpec(shape, _const2)
    return pl.pallas_call(
        _mla_in_kernel,
        out_shape=(q_lat, head, vt, jax.ShapeDtypeStruct((BATCH, N_TOK, MLA_MIX), F32)),
        grid=(BATCH, N_BLOCKS),
        in_specs=[
            pl.BlockSpec((1, TOK_BLOCK, D_MODEL), _tok_map),
            pl.BlockSpec((1, 3, D_MODEL), _mod_map),
            row(D_MODEL),
            full(D_MODEL, ODD_IN_PAD),
            row(Q_LORA),
            full(MLA_HEADS * HEAD_PAD, Q_LORA),
            row(KV_LORA),
            full(KV_LORA, MLA_HEADS * HEAD_PAD),
            full(MLA_MIX, KV_LORA),
            full(HEAD_PAD, TOK_BLOCK), row(SEG_TILE), row(HEAD_PAD),
            full(SEG_TILE, SEG_TILE),
            tab_spec, tab_spec, tab_spec, tab_t_spec, tab_t_spec,
        ],
        out_specs=(q_spec, head_spec, vt_spec, pl.BlockSpec((1, TOK_BLOCK, MLA_MIX), _tok_map)),
        compiler_params=_params("arbitrary", "arbitrary"),
        name="mla_in",
    )(h, mod, norm_w, w_in, q_a_norm, w_uq_t, kv_a_norm, w_uk, w_uv_t, gq_t, gkn, gkr, seg_k,
      cos_t, sin_lo, sin_hi, cos_q, sin_q)


def _attention_kernel(q_ref, k_ref, vt_ref, o_ref, s_ref, ot_ref):
    n_chunks = N_TOK // KV_CHUNK
    groups = KV_CHUNK // SUBLANES
    n_qb = SEQ // Q_BLOCK
    assert HEADS_PER_STEP % 2 == 0

    def q_rows(qb):
        return pl.ds(pl.multiple_of(qb * Q_BLOCK, Q_BLOCK), Q_BLOCK)

    def chunk(c):
        return slice(c * KV_CHUNK, (c + 1) * KV_CHUNK)

    def stages(score, value, mx_prev):
        mx = m_val = None
        if value is not None:
            m_val = jnp.max(mx_prev, axis=0, keepdims=True)
        for c in range(n_chunks):
            if score is not None:
                qb, hh = score
                sc = _dot(k_ref[0, hh, chunk(c), :], q_ref[0, hh, qb])
                s_ref[hh % 2, chunk(c), :] = sc
                cm = jnp.max(sc.reshape(groups, SUBLANES, Q_BLOCK), axis=0)
                mx = cm if mx is None else jnp.maximum(mx, cm)
            if value is not None:
                qb, hh = value
                e = jnp.exp2(s_ref[hh % 2, chunk(c), :] - m_val)
                pv = _dot(vt_ref[0, hh, c], e.astype(BF16))
                acc = pv if c == 0 else acc + pv
        if value is not None:
            qb, hh = value
            inv = 1.0 / acc[V_HEAD:V_HEAD + 1, :]
            ot_ref[hh * V_HEAD:(hh + 1) * V_HEAD, :] = acc[0:V_HEAD, :] * inv
            if hh == HEADS_PER_STEP - 1:
                o_ref[0, q_rows(qb), :] = ot_ref[...].T
        return mx

    def within_block(qb, mx):
        for hh in range(HEADS_PER_STEP - 1):
            mx = stages((qb, hh + 1), (qb, hh), mx)
        return mx

    def block(qb, mx):
        return stages((qb + 1, 0), (qb, HEADS_PER_STEP - 1), within_block(qb, mx))

    mx = stages((0, 0), None, None)
    mx = lax.fori_loop(0, n_qb - 1, block, mx)
    mx = within_block(n_qb - 1, mx)
    stages(None, (n_qb - 1, HEADS_PER_STEP - 1), mx)


def _attention(q, k, vt):
    pairs = MLA_HEADS // HEADS_PER_STEP
    pair_map = lambda b, hp: (b, hp, 0, 0)
    return pl.pallas_call(
        _attention_kernel,
        out_shape=jax.ShapeDtypeStruct((BATCH, SEQ, MLA_MIX), F32),
        grid=(BATCH, pairs),
        in_specs=[
            pl.BlockSpec((1, HEADS_PER_STEP, SEQ // Q_BLOCK, HEAD_PAD, Q_BLOCK), lambda b, hp: (b, hp, 0, 0, 0)),
            pl.BlockSpec((1, HEADS_PER_STEP, N_TOK, HEAD_PAD), pair_map),
            pl.BlockSpec((1, HEADS_PER_STEP, N_TOK // KV_CHUNK, V_ROWS, KV_CHUNK), lambda b, hp: (b, hp, 0, 0, 0)),
        ],
        out_specs=pl.BlockSpec((1, SEQ, HEADS_PER_STEP * V_HEAD), lambda b, hp: (b, 0, hp)),
        scratch_shapes=[
            pltpu.VMEM((2, N_TOK, Q_BLOCK), F32),
            pltpu.VMEM((HEADS_PER_STEP * V_HEAD, Q_BLOCK), F32),
        ],
        compiler_params=_params("arbitrary", "arbitrary"),
        name="attention",
    )(q, k, vt)


OUT_PARTS = 4


def _mla_out_kernel(att_ref, *refs):
    gate_refs, h_refs = refs[:OUT_PARTS], refs[OUT_PARTS:2 * OUT_PARTS]
    mod_ref, wo_ref, o_ref = refs[2 * OUT_PARTS:]
    for part in range(OUT_PARTS):
        rows = slice(part * TOK_BLOCK, (part + 1) * TOK_BLOCK)
        mix = (att_ref[0, rows, :] * gate_refs[part][0]).astype(BF16)
        o_ref[0, rows, :] = h_refs[part][0] + mod_ref[0, 2:3, :] * _dot(mix, wo_ref[...])


def _mla_out(att, gate, h, mod, w_out):
    lat_part = lambda part: (lambda b, t: (b, t * OUT_PARTS + part + CTX_BLOCKS, 0))
    wide = pl.BlockSpec((1, OUT_PARTS * TOK_BLOCK, D_MODEL), _tok_map)
    return pl.pallas_call(
        _mla_out_kernel,
        out_shape=jax.ShapeDtypeStruct((BATCH, SEQ, D_MODEL), F32),
        grid=(BATCH, SEQ // (OUT_PARTS * TOK_BLOCK)),
        in_specs=[wide]
        + [pl.BlockSpec((1, TOK_BLOCK, MLA_MIX), lat_part(part)) for part in range(OUT_PARTS)]
        + [pl.BlockSpec((1, TOK_BLOCK, D_MODEL), lat_part(part)) for part in range(OUT_PARTS)]
        + [pl.BlockSpec((1, 3, D_MODEL), lambda b, t: (b, 0, 0)),
           pl.BlockSpec((MLA_MIX, D_MODEL), _const2)],
        out_specs=wide,
        compiler_params=_params("arbitrary", "arbitrary"),
        name="mla_out",
    )(att, *([gate] * OUT_PARTS), *([h] * OUT_PARTS), mod, w_out)


def _pad_heads(w, width, offset=0):
    k = w.shape[0]
    w = w.reshape(k, MLA_HEADS, width)
    return jnp.pad(w, ((0, 0), (0, 0), (offset, HEAD_PAD - width - offset))).reshape(k, MLA_HEADS * HEAD_PAD)


def _mla_weights(w_in, w_uq, w_ukv, q_norm, k_norm):
    cuts = [Q_LORA, Q_LORA + KV_LORA, Q_LORA + KV_LORA + QK_ROPE]
    w_kr = jnp.pad(w_in[:, cuts[1]:cuts[2]], ((0, 0), (QK_NOPE, LANES - QK_DIM)))
    w_in_p = jnp.concatenate([w_in[:, :cuts[1]], w_kr, w_in[:, cuts[2]:]], axis=1).astype(BF16)
    w_uq_t = _pad_heads(w_uq, QK_DIM).T.astype(BF16)
    w_kv = w_ukv.reshape(KV_LORA, MLA_HEADS, QK_NOPE + V_HEAD)
    w_uk = _pad_heads(w_kv[:, :, :QK_NOPE].reshape(KV_LORA, -1), QK_NOPE).astype(BF16)
    w_uv_t = w_kv[:, :, QK_NOPE:].reshape(KV_LORA, MLA_MIX).T.astype(BF16)
    pad_row = lambda v, off: jnp.pad(v, (off, HEAD_PAD - v.shape[0] - off)).reshape(1, HEAD_PAD)
    gq_t = jnp.broadcast_to(pad_row(q_norm, 0).reshape(HEAD_PAD, 1), (HEAD_PAD, TOK_BLOCK))
    gkn = jnp.tile(pad_row(k_norm[:QK_NOPE], 0), (1, SEG_TILE // HEAD_PAD))
    gkr = pad_row(k_norm[QK_NOPE:], QK_NOPE)
    return w_in_p, w_uq_t, w_uk, w_uv_t, gq_t, gkn, gkr


def _key_segment_matrix():
    lane = np.arange(SEG_TILE)
    is_nope = lane % HEAD_PAD < QK_NOPE
    same_head = (lane // HEAD_PAD)[:, None] == (lane // HEAD_PAD)[None, :]
    return jnp.asarray(same_head & is_nope[:, None] & is_nope[None, :], dtype=BF16)


def _rope_tables():
    f32 = np.float32
    rows = SEQ // GRID_W
    row = np.repeat(np.arange(rows, dtype=f32), GRID_W)
    col = np.tile(np.arange(GRID_W, dtype=f32), rows)
    n_freq = QK_ROPE // 4
    inv = np.power(f32(ROPE_BASE), -np.arange(n_freq, dtype=f32) / f32(n_freq)).astype(f32)
    ang = np.concatenate([row[:, None] * inv, col[:, None] * inv], axis=-1).astype(f32)
    cos, sin = np.cos(ang).astype(f32), np.sin(ang).astype(f32)
    half = QK_ROPE // 2
    zeros = np.zeros_like(cos)
    pad_l = np.ones((SEQ, QK_NOPE), f32)
    tail = np.zeros((SEQ, HEAD_PAD - QK_DIM), f32)
    cos_t = np.concatenate([pad_l, cos, cos, tail], axis=-1)
    sin_lo = np.concatenate([0 * pad_l, -sin, zeros, tail], axis=-1)
    sin_hi = np.concatenate([0 * pad_l, zeros, sin, tail], axis=-1)
    ctx_cos = np.concatenate([np.ones((CTX_LEN, QK_DIM), f32), np.zeros((CTX_LEN, HEAD_PAD - QK_DIM), f32)], -1)
    ctx_zero = np.zeros((CTX_LEN, HEAD_PAD), f32)
    cos_q = np.concatenate([np.ones((half, CTX_LEN), f32), cos.T], axis=1)
    sin_q = np.concatenate([np.zeros((half, CTX_LEN), f32), sin.T], axis=1)
    tables = (np.concatenate([ctx_cos, cos_t], 0), np.concatenate([ctx_zero, sin_lo], 0),
              np.concatenate([ctx_zero, sin_hi], 0), cos_q, sin_q)
    return tuple(jnp.asarray(t) for t in tables)


def kernel(x, c, ctx, c_ctx, mod_w, mod_b, norm_w, e_w_in, e_conv_w, e_lam_re, e_lam_im, e_log_step, e_b_re, e_b_im, e_c_re, e_c_im, e_d, e_glu_w, e_glu_b, e_w_out, o_w_in, o_q_a_norm, o_w_uq, o_kv_a_norm, o_w_ukv, o_q_norm, o_k_norm, o_w_out):
    cond = jnp.zeros((COND_ROWS, D_MODEL), F32).at[:BATCH].set(c).at[CTX_ROW].set(c_ctx)
    mod = _modulation(cond, mod_w, mod_b)[:, :CTX_ROW + 1].reshape(2, CTX_ROW + 1, 3, D_MODEL)

    mod_even = jnp.stack([mod[0, :BATCH], jnp.broadcast_to(mod[0, CTX_ROW], (BATCH, 3, D_MODEL))])
    perm, perm_t = _time_major_perm()
    p, q, zs, u, u_tm = _even_in(ctx, x, mod_even, norm_w[0:1], e_w_in[0].astype(BF16), perm)
    a_re, a_im, bbr, bbi = _s5_discretize(e_lam_re[0], e_lam_im[0], e_log_step[0], e_b_re[0], e_b_im[0])
    w_b, c_re, c_im = _s5_matrices(bbr, bbi, e_c_re[0], e_c_im[0])
    bcast = lambda a: jnp.broadcast_to(a[:, None, :], (2, BATCH, N_STATE))
    scan_args = (u_tm, w_b, c_re, c_im, bcast(a_re), bcast(a_im))
    y = _s5_scan(1, *scan_args, y_prev=_s5_scan(0, *scan_args))
    h1 = _even_out(y, u, p, q, zs, ctx, x, mod_even, e_d[0:1], e_glu_w[0].astype(BF16), e_glu_b[0:1],
                   e_conv_w[0], e_w_out[0].astype(BF16), perm_t)

    w_in_p, w_uq_t, w_uk, w_uv_t, gq_t, gkn, gkr = _mla_weights(o_w_in[0], o_w_uq[0], o_w_ukv[0],
                                                                o_q_norm[0], o_k_norm[0])
    cos_t, sin_lo, sin_hi, cos_q, sin_q = _rope_tables()
    qh, kh, vh, gate = _mla_in(h1, mod[1], norm_w[1:2], w_in_p, o_q_a_norm[0:1], w_uq_t, o_kv_a_norm[0:1],
                               w_uk, w_uv_t, gq_t, gkn, gkr, _key_segment_matrix(),
                               cos_t, sin_lo, sin_hi, cos_q, sin_q)
    att = _attention(qh, kh, vh)
    return _mla_out(att, gate, h1, mod[1], o_w_out[0].astype(BF16))
```

```python
import functools
import math

import jax
import jax.numpy as jnp
import numpy as np
from jax import lax
from jax.experimental import pallas as pl
from jax.experimental.pallas import tpu as pltpu

F32 = jnp.float32
BF16 = jnp.bfloat16

D_MODEL = 1024
BATCH = 8
SEQ = 4096
CTX_LEN = 256
GRID_W = 64
EPS = 1e-6
CONV_WIDTH = 512
SSM_WIDTH = 512
SSM_GROUP = 16
SSM_GROUPS = SSM_WIDTH // SSM_GROUP
SSM_STATE = 64
MLA_HEADS = 16
QK_NOPE = 64
QK_ROPE = 32
QK_DIM = QK_NOPE + QK_ROPE
V_HEAD = 64
Q_LORA = 384
KV_LORA = 256
MLA_MIX = MLA_HEADS * V_HEAD
ROPE_BASE = 10000.0

LANES = 128
SUBLANES = 8
VMEM_BYTES_V7X = 64 * 1024 * 1024
VMEM_LIMIT_BYTES = VMEM_BYTES_V7X * 7 // 8

N_TOK = CTX_LEN + SEQ
TOK_BLOCK = 256
N_BLOCKS = N_TOK // TOK_BLOCK
CTX_BLOCKS = CTX_LEN // TOK_BLOCK
assert CTX_BLOCKS == 1 and N_TOK % TOK_BLOCK == 0
COND_ROWS = 2 * SUBLANES
CTX_ROW = BATCH
EVEN_IN = 4 * CONV_WIDTH + 2 * SSM_WIDTH

SCAN_STEPS = 64
SCAN_ROWS = SCAN_STEPS * BATCH
SCAN_CHUNKS = N_TOK // SCAN_STEPS
SCAN_CTX_CHUNKS = CTX_LEN // SCAN_STEPS
SLAB_GROUPS = LANES // SSM_GROUP
N_SLABS = SSM_GROUPS // SLAB_GROUPS
SLAB_STATE = SLAB_GROUPS * SSM_STATE
N_STATE = SSM_GROUPS * SSM_STATE

HEAD_PAD = LANES
SEG_TILE = 2 * LANES
Q_BLOCK = 512
HEADS_PER_STEP = 4
KV_CHUNK = 256
V_ROWS = V_HEAD + 2 * SUBLANES
assert KV_CHUNK == TOK_BLOCK
ODD_IN_PAD = Q_LORA + KV_LORA + LANES + MLA_MIX
SOFTMAX_SCALE = QK_DIM ** -0.5 * math.log2(math.e)


def _params(*sem):
    return pltpu.CompilerParams(dimension_semantics=sem, vmem_limit_bytes=VMEM_LIMIT_BYTES)


def _dot(a, b):
    return jnp.dot(a, b, preferred_element_type=F32)


def _split_bf16(a):
    hi = a.astype(BF16)
    lo = (a - hi.astype(F32)).astype(BF16)
    return hi, lo


def _dot3(a, b):
    ah, al = _split_bf16(a)
    bh, bl = _split_bf16(b)
    return _dot(ah, bh) + (_dot(ah, bl) + _dot(al, bh))


def _rms(x, w):
    return x * lax.rsqrt(jnp.mean(x * x, axis=-1, keepdims=True) + EPS) * w


def _modulated_norm(h, norm_w, mod):
    return _rms(h, norm_w) * (1.0 + mod[1:2, :]) + mod[0:1, :]


def _mod_kernel(cond_ref, w_ref, b_ref, o_ref):
    cond = cond_ref[...]
    o_ref[0] = _dot3(cond * jax.nn.sigmoid(cond), w_ref[0]) + b_ref[0]


def _modulation(cond, mod_w, mod_b):
    depth = mod_w.shape[0]
    n_col = 3
    return pl.pallas_call(
        _mod_kernel,
        out_shape=jax.ShapeDtypeStruct((depth, COND_ROWS, 3 * D_MODEL), F32),
        grid=(depth, n_col),
        in_specs=[
            pl.BlockSpec((COND_ROWS, D_MODEL), lambda i, j: (0, 0)),
            pl.BlockSpec((1, D_MODEL, D_MODEL), lambda i, j: (i, 0, j)),
            pl.BlockSpec((1, 1, D_MODEL), lambda i, j: (i, 0, j)),
        ],
        out_specs=pl.BlockSpec((1, COND_ROWS, D_MODEL), lambda i, j: (i, 0, j)),
        compiler_params=_params("arbitrary", "arbitrary"),
        name="modulation",
    )(cond, mod_w, mod_b.reshape(depth, 1, 3 * D_MODEL))


def _tok_map(b, t):
    return (b, t, 0)


def _mod_map(b, t):
    return (jnp.where(t < CTX_BLOCKS, CTX_ROW, b), 0, 0)


def _const2(b, t):
    return (0, 0)


EVEN_ROWS = 1024
TIME_BLOCK = EVEN_ROWS // BATCH
TIME_BLOCKS = N_TOK // TIME_BLOCK
CTX_TIME_BLOCKS = CTX_LEN // TIME_BLOCK
assert N_TOK % TIME_BLOCK == 0 and CTX_LEN % TIME_BLOCK == 0


PERM_ROWS = 256
PERM_STEPS = PERM_ROWS // BATCH
PERM_PARTS = EVEN_ROWS // PERM_ROWS


def _time_major_perm():
    r = np.arange(PERM_ROWS)
    src = (r % BATCH) * PERM_STEPS + r // BATCH
    perm = src[:, None] == np.arange(PERM_ROWS)[None, :]
    return jnp.asarray(perm, dtype=BF16), jnp.asarray(perm.T, dtype=BF16)


def _time_part(v, part):
    v3 = v.reshape(BATCH, TIME_BLOCK, v.shape[-1])
    return v3[:, part * PERM_STEPS:(part + 1) * PERM_STEPS, :].reshape(PERM_ROWS, v.shape[-1])


def _ctx_time_map(i):
    return (0, jnp.minimum(i, CTX_TIME_BLOCKS - 1), 0)


def _lat_time_map(i):
    return (0, jnp.maximum(i - CTX_TIME_BLOCKS, 0), 0)


def _tok_time_map(i):
    return (0, i, 0)


def _mod_time_map(i):
    return (jnp.where(i < CTX_TIME_BLOCKS, 1, 0), 0, 0, 0)


def _const1(i):
    return (0, 0)


def _pick_hidden_time(ctx_ref, x_ref):
    return jnp.where(pl.program_id(0) < CTX_TIME_BLOCKS, ctx_ref[...], x_ref[...])


def _even_in_kernel(ctx_ref, x_ref, mod_ref, nw_ref, w_ref, perm_ref, p_ref, q_ref, zs_ref, u_ref, utm_ref):
    h = _pick_hidden_time(ctx_ref, x_ref)
    mod = mod_ref[0]
    a = _rms(h, nw_ref[...]) * (1.0 + mod[:, 1:2, :]) + mod[:, 0:1, :]
    r = _dot(a.reshape(EVEN_ROWS, D_MODEL).astype(BF16), w_ref[...])
    cw = CONV_WIDTH
    xa, ba, ca, za = (r[:, i * cw:(i + 1) * cw] for i in range(4))
    us = r[:, 4 * cw:4 * cw + SSM_WIDTH]
    zs = r[:, 4 * cw + SSM_WIDTH:]
    blk = (BATCH, TIME_BLOCK, CONV_WIDTH)
    p_ref[...] = (ca * xa).reshape(blk)
    q_ref[...] = (ba * (za * jax.nn.sigmoid(za))).reshape(blk)
    zs_ref[...] = (zs * jax.nn.sigmoid(zs)).reshape(blk)
    u_ref[...] = us.reshape(blk)
    ub = us.astype(BF16)
    for part in range(PERM_PARTS):
        utm_ref[part * PERM_ROWS:(part + 1) * PERM_ROWS, :] = _dot(perm_ref[...], _time_part(ub, part)).astype(BF16)


def _even_in(ctx, x, mod, norm_w, w_in, perm):
    tok = jax.ShapeDtypeStruct((BATCH, N_TOK, CONV_WIDTH), F32)
    tok_spec = pl.BlockSpec((BATCH, TIME_BLOCK, CONV_WIDTH), _tok_time_map)
    return pl.pallas_call(
        _even_in_kernel,
        out_shape=(tok, tok, tok, tok, jax.ShapeDtypeStruct((N_TOK * BATCH, SSM_WIDTH), BF16)),
        grid=(TIME_BLOCKS,),
        in_specs=[
            pl.BlockSpec((BATCH, TIME_BLOCK, D_MODEL), _ctx_time_map),
            pl.BlockSpec((BATCH, TIME_BLOCK, D_MODEL), _lat_time_map),
            pl.BlockSpec((1, BATCH, 3, D_MODEL), _mod_time_map),
            pl.BlockSpec((1, D_MODEL), _const1),
            pl.BlockSpec((D_MODEL, EVEN_IN), _const1, pipeline_mode=pl.Buffered(1)),
            pl.BlockSpec((PERM_ROWS, PERM_ROWS), _const1),
        ],
        out_specs=(tok_spec, tok_spec, tok_spec, tok_spec,
                   pl.BlockSpec((EVEN_ROWS, SSM_WIDTH), lambda i: (i, 0))),
        compiler_params=_params("arbitrary"),
        name="even_in",
    )(ctx, x, mod, norm_w, w_in, perm)


def _s5_disc_kernel(lr_ref, li_ref, ls_ref, br_ref, bi_ref, ar_ref, ai_ref, bbr_ref, bbi_ref):
    lr = lr_ref[...]
    li = li_ref[...]
    dt = jnp.exp(ls_ref[...])
    mag = jnp.exp(lr * dt)
    ar = mag * jnp.cos(li * dt)
    ai = mag * jnp.sin(li * dt)
    nr = ar - 1.0
    den = lr * lr + li * li
    fr = (nr * lr + ai * li) / den
    fi = (ai * lr - nr * li) / den
    ar_ref[...] = ar
    ai_ref[...] = ai
    br = br_ref[...]
    bi = bi_ref[...]
    bbr_ref[...] = fr[:, None, :] * br - fi[:, None, :] * bi
    bbi_ref[...] = fr[:, None, :] * bi + fi[:, None, :] * br


def _s5_discretize(lam_re, lam_im, log_step, b_re, b_im):
    flat = lambda a: a.reshape(2, N_STATE)
    chan = lambda a: jnp.transpose(a, (0, 3, 1, 2)).reshape(2, SSM_GROUP, N_STATE)
    ls = jnp.repeat(log_step, SSM_STATE, axis=-1)
    vec = jax.ShapeDtypeStruct((2, N_STATE), F32)
    mat = jax.ShapeDtypeStruct((2, SSM_GROUP, N_STATE), F32)
    return pl.pallas_call(_s5_disc_kernel, out_shape=(vec, vec, mat, mat), name="s5_discretize")(
        flat(lam_re), flat(lam_im), ls, chan(b_re), chan(b_im))


def _block_diag_groups(a):
    d, j, g, r, c = a.shape
    eye = jnp.eye(g, dtype=bool)[None, None, :, None, :, None]
    full = jnp.where(eye, a[:, :, :, :, None, :], jnp.zeros((), a.dtype))
    return full.reshape(d, j, g * r, g * c)


def _s5_matrices(bbr, bbi, c_re, c_im):
    def in_map(bb):
        a = bb.reshape(2, SSM_GROUP, N_SLABS, SLAB_GROUPS, SSM_STATE)
        return _block_diag_groups(jnp.transpose(a, (0, 2, 3, 1, 4)))
    w_b = jnp.concatenate([in_map(bbr), in_map(bbi)], axis=-1).astype(BF16)

    def out_map(cc):
        a = cc.reshape(2, N_SLABS, SLAB_GROUPS, SSM_GROUP, SSM_STATE)
        return _block_diag_groups(jnp.transpose(a, (0, 1, 2, 4, 3))).astype(BF16)
    return w_b, out_map(c_re), out_map(c_im)


def _scan_chunk(d, i):
    back = jnp.where(i < SCAN_CTX_CHUNKS, SCAN_CTX_CHUNKS - 1 - i,
                     SCAN_CHUNKS - 1 - (i - SCAN_CTX_CHUNKS))
    return jnp.where(d == 0, i, back)


SCAN_BUFFERS = 3


def _s5_scan_kernel(backward, ucur_ref, unext_ref, wb_ref, cre_ref, cim_ref, ar_ref, ai_ref, *rest):
    yprev_ref = rest[0] if backward else None
    y_ref = rest[1] if backward else rest[0]
    scratch = rest[2:] if backward else rest[1:]
    bufs, st_ref = scratch[:SCAN_BUFFERS], scratch[SCAN_BUFFERS]
    i = pl.program_id(0)
    width = 2 * SLAB_STATE
    n_piece = 2 * N_SLABS

    def input_map(u_ref, buf):
        ub = u_ref[...]
        for j in range(N_SLABS):
            buf[:, j * width:(j + 1) * width] = _dot(ub[:, j * LANES:(j + 1) * LANES], wb_ref[0, j])

    def recurrence(buf):
        state = [st_ref[:, p * SLAB_STATE:(p + 1) * SLAB_STATE] for p in range(n_piece)]
        for k in range(SCAN_STEPS):
            t = SCAN_STEPS - 1 - k if backward else k
            rows = slice(t * BATCH, (t + 1) * BATCH)
            for j in range(N_SLABS):
                s_re, s_im = state[2 * j], state[2 * j + 1]
                a_re = ar_ref[0, :, j * SLAB_STATE:(j + 1) * SLAB_STATE]
                a_im = ai_ref[0, :, j * SLAB_STATE:(j + 1) * SLAB_STATE]
                re_cols = slice(j * width, j * width + SLAB_STATE)
                im_cols = slice(j * width + SLAB_STATE, (j + 1) * width)
                n_re = a_re * s_re - a_im * s_im + buf[rows, re_cols]
                n_im = a_re * s_im + a_im * s_re + buf[rows, im_cols]
                buf[rows, re_cols] = n_re
                buf[rows, im_cols] = n_im
                state[2 * j], state[2 * j + 1] = n_re, n_im
        for p in range(n_piece):
            st_ref[:, p * SLAB_STATE:(p + 1) * SLAB_STATE] = state[p]

    def readout(buf):
        for j in range(N_SLABS):
            s_re = buf[:, j * width:j * width + SLAB_STATE].astype(BF16)
            s_im = buf[:, j * width + SLAB_STATE:(j + 1) * width].astype(BF16)
            cols = slice(j * LANES, (j + 1) * LANES)
            y = _dot(s_re, cre_ref[0, j]) - _dot(s_im, cim_ref[0, j])
            y_ref[:, cols] = yprev_ref[:, cols] + y if backward else y

    @pl.when(i == 0)
    def _():
        st_ref[...] = jnp.zeros_like(st_ref)
        input_map(ucur_ref, bufs[0])
        bufs[SCAN_BUFFERS - 1][...] = jnp.zeros_like(bufs[SCAN_BUFFERS - 1])

    for r in range(SCAN_BUFFERS):
        @pl.when(lax.rem(i, SCAN_BUFFERS) == r)
        def _(r=r):
            input_map(unext_ref, bufs[(r + 1) % SCAN_BUFFERS])
            recurrence(bufs[r])
            readout(bufs[(r + SCAN_BUFFERS - 1) % SCAN_BUFFERS])


def _s5_scan(direction, u_tm, w_b, c_re, c_im, a_re, a_im, y_prev=None):
    backward = direction == 1
    dir_map = lambda i: (direction, 0, 0, 0)
    vec_map = lambda i: (direction, 0, 0)
    last = SCAN_CHUNKS - 1
    chunk = lambda i: _scan_chunk(direction, jnp.clip(i, 0, last))
    y_spec = pl.BlockSpec((SCAN_ROWS, SSM_WIDTH), lambda i: (chunk(i - 1), 0))
    return pl.pallas_call(
        functools.partial(_s5_scan_kernel, backward),
        out_shape=jax.ShapeDtypeStruct((N_TOK * BATCH, SSM_WIDTH), F32),
        grid=(SCAN_CHUNKS + 1,),
        in_specs=[
            pl.BlockSpec((SCAN_ROWS, SSM_WIDTH), lambda i: (chunk(i), 0)),
            pl.BlockSpec((SCAN_ROWS, SSM_WIDTH), lambda i: (chunk(i + 1), 0)),
            pl.BlockSpec((1, N_SLABS, LANES, 2 * SLAB_STATE), dir_map),
            pl.BlockSpec((1, N_SLABS, SLAB_STATE, LANES), dir_map),
            pl.BlockSpec((1, N_SLABS, SLAB_STATE, LANES), dir_map),
            pl.BlockSpec((1, BATCH, N_STATE), vec_map),
            pl.BlockSpec((1, BATCH, N_STATE), vec_map),
        ] + ([y_spec] if backward else []),
        out_specs=y_spec,
        scratch_shapes=[pltpu.VMEM((SCAN_ROWS, 2 * N_STATE), F32) for _ in range(SCAN_BUFFERS)] + [
            pltpu.VMEM((BATCH, 2 * N_STATE), F32),
        ],
        input_output_aliases={7: 0} if backward else {},
        compiler_params=_params("arbitrary"),
        name="s5_scan_bwd" if backward else "s5_scan_fwd",
    )(u_tm, u_tm, w_b, c_re, c_im, a_re, a_im, *([y_prev] if backward else []))


HALO_BLOCKS = TIME_BLOCK // SUBLANES


def _even_out_kernel(y_ref, u_ref, p_ref, pprev_ref, pnext_ref, q_ref, zs_ref, ctx_ref, x_ref, mod_ref,
                     d_ref, gw_ref, gb_ref, cw_ref, wo_ref, perm_ref, o_ref):
    i = pl.program_id(0)
    flat = lambda ref: ref[...].reshape(EVEN_ROWS, ref.shape[-1])
    y_hi, y_lo = _split_bf16(y_ref[...])
    parts = []
    for part in range(PERM_PARTS):
        rows = slice(part * PERM_ROWS, (part + 1) * PERM_ROWS)
        y_part = _dot(perm_ref[...], y_hi[rows]) + _dot(perm_ref[...], y_lo[rows])
        parts.append(y_part.reshape(BATCH, PERM_STEPS, SSM_WIDTH))
    y = jnp.concatenate(parts, axis=1).reshape(EVEN_ROWS, SSM_WIDTH) + d_ref[...] * flat(u_ref)
    g = jax.nn.gelu(y)
    z = _dot(g.astype(BF16), gw_ref[...]) + gb_ref[...]
    ssm = g * jax.nn.sigmoid(z) * flat(zs_ref)

    p = flat(p_ref)
    t_in_block = lax.broadcasted_iota(jnp.int32, p.shape, 0) % TIME_BLOCK
    prev_ok = jnp.logical_and(i != 0, i != CTX_TIME_BLOCKS).astype(F32)
    next_ok = jnp.logical_and(i != CTX_TIME_BLOCKS - 1, i != TIME_BLOCKS - 1).astype(F32)
    halo_shape = (BATCH, TIME_BLOCK, CONV_WIDTH)
    halo_prev = jnp.broadcast_to(pprev_ref[:, SUBLANES - 1:SUBLANES, :] * prev_ok, halo_shape)
    halo_next = jnp.broadcast_to(pnext_ref[:, 0:1, :] * next_ok, halo_shape)
    p_prev = jnp.where(t_in_block == 0, halo_prev.reshape(p.shape), pltpu.roll(p, 1, 0))
    p_next = jnp.where(t_in_block == TIME_BLOCK - 1, halo_next.reshape(p.shape), pltpu.roll(p, EVEN_ROWS - 1, 0))
    cw = cw_ref[...]
    conv = flat(q_ref) * (p_prev * cw[0:1, :] + p * cw[1:2, :] + p_next * cw[2:3, :])

    out = _dot(conv.astype(BF16), wo_ref[0:CONV_WIDTH, :]) + _dot(ssm.astype(BF16), wo_ref[CONV_WIDTH:, :])
    gate = mod_ref[0][:, 2:3, :]
    o_ref[...] = _pick_hidden_time(ctx_ref, x_ref) + gate * out.reshape(BATCH, TIME_BLOCK, D_MODEL)


def _even_out(y_tm, u, p, q, zs, ctx, x, mod, d_skip, glu_w, glu_b, conv_w, w_out, perm_t):
    halo_prev = lambda i: (0, jnp.maximum(i * HALO_BLOCKS - 1, 0), 0)
    halo_next = lambda i: (0, jnp.minimum((i + 1) * HALO_BLOCKS, N_TOK // SUBLANES - 1), 0)
    tok_spec = pl.BlockSpec((BATCH, TIME_BLOCK, CONV_WIDTH), _tok_time_map)
    halo = lambda m: pl.BlockSpec((BATCH, SUBLANES, CONV_WIDTH), m)
    return pl.pallas_call(
        _even_out_kernel,
        out_shape=jax.ShapeDtypeStruct((BATCH, N_TOK, D_MODEL), F32),
        grid=(TIME_BLOCKS,),
        in_specs=[
            pl.BlockSpec((EVEN_ROWS, SSM_WIDTH), lambda i: (i, 0)),
            tok_spec, tok_spec, halo(halo_prev), halo(halo_next), tok_spec, tok_spec,
            pl.BlockSpec((BATCH, TIME_BLOCK, D_MODEL), _ctx_time_map),
            pl.BlockSpec((BATCH, TIME_BLOCK, D_MODEL), _lat_time_map),
            pl.BlockSpec((1, BATCH, 3, D_MODEL), _mod_time_map),
            pl.BlockSpec((1, SSM_WIDTH), _const1),
            pl.BlockSpec((SSM_WIDTH, SSM_WIDTH), _const1),
            pl.BlockSpec((1, SSM_WIDTH), _const1),
            pl.BlockSpec((3, CONV_WIDTH), _const1),
            pl.BlockSpec((CONV_WIDTH + SSM_WIDTH, D_MODEL), _const1, pipeline_mode=pl.Buffered(1)),
            pl.BlockSpec((PERM_ROWS, PERM_ROWS), _const1),
        ],
        out_specs=pl.BlockSpec((BATCH, TIME_BLOCK, D_MODEL), _tok_time_map),
        compiler_params=_params("arbitrary"),
        name="even_out",
    )(y_tm, u, p, p, p, q, zs, ctx, x, mod, d_skip, glu_w, glu_b, conv_w, w_out, perm_t)


def _rope(v, cos_t, sin_lo, sin_hi):
    half = QK_ROPE // 2
    return v * cos_t + pltpu.roll(v, LANES - half, 1) * sin_lo + pltpu.roll(v, half, 1) * sin_hi


_NT = (((1,), (1,)), ((), ()))


def _mla_in_kernel(h_ref, mod_ref, nw_ref, win_ref, qan_ref, wuqt_ref, kvan_ref, wuk_ref, wuvt_ref,
                   gqt_ref, gkn_ref, gkr_ref, segk_ref, cos_ref, slo_ref, shi_ref, cost_ref, sint_ref,
                   q_ref, k_ref, vt_ref, gate_ref):
    a = _modulated_norm(h_ref[0], nw_ref[...], mod_ref[0])
    r = _dot(a.astype(BF16), win_ref[...])
    cq = r[:, :Q_LORA]
    ckv = r[:, Q_LORA:Q_LORA + KV_LORA]
    kr = r[:, Q_LORA + KV_LORA:Q_LORA + KV_LORA + LANES]
    gate = r[:, Q_LORA + KV_LORA + LANES:]
    gate_ref[0] = gate * jax.nn.sigmoid(gate)

    cqn = _rms(cq, qan_ref[...]).astype(BF16)
    ckvn = _rms(ckv, kvan_ref[...]).astype(BF16)
    qt = lax.dot_general(wuqt_ref[...], cqn, _NT, preferred_element_type=F32)
    kf = _dot(ckvn, wuk_ref[...])
    vt = lax.dot_general(wuvt_ref[...], ckvn, _NT, preferred_element_type=F32)
    ones = jnp.ones((MLA_HEADS, V_ROWS - V_HEAD, TOK_BLOCK), BF16)
    vt_ref[0, :, 0] = jnp.concatenate([vt.astype(BF16).reshape(MLA_HEADS, V_HEAD, TOK_BLOCK), ones], axis=1)

    half = QK_ROPE // 2
    cos_q, sin_q = cost_ref[...], sint_ref[...]
    g_nope, g_rope = gqt_ref[0:QK_NOPE, :], gqt_ref[QK_NOPE:QK_DIM, :]
    pad_rows = jnp.zeros((HEAD_PAD - QK_DIM, TOK_BLOCK), F32)
    for h in range(MLA_HEADS):
        qh = qt[h * HEAD_PAD:(h + 1) * HEAD_PAD, :]
        sq = qh * qh
        ms_nope = jnp.sum(sq[0:QK_NOPE], axis=0, keepdims=True) * (1.0 / QK_NOPE)
        ms_rope = jnp.sum(sq[QK_NOPE:QK_DIM], axis=0, keepdims=True) * (1.0 / QK_ROPE)
        nope = qh[0:QK_NOPE] * lax.rsqrt(ms_nope + EPS) * g_nope
        rope = qh[QK_NOPE:QK_DIM] * lax.rsqrt(ms_rope + EPS) * g_rope
        x1, x2 = rope[0:half], rope[half:]
        tile = jnp.concatenate([nope, x1 * cos_q - x2 * sin_q, x1 * sin_q + x2 * cos_q, pad_rows], axis=0)
        q_ref[0, h, 0] = (tile * SOFTMAX_SCALE).astype(BF16)

    cos_t, sin_lo, sin_hi = cos_ref[...], slo_ref[...], shi_ref[...]
    kr_ms = jnp.sum(kr * kr, axis=-1, keepdims=True) * (1.0 / QK_ROPE)
    k_rope = _rope(kr * lax.rsqrt(kr_ms + EPS) * gkr_ref[...], cos_t, sin_lo, sin_hi)
    for j in range(MLA_HEADS * HEAD_PAD // SEG_TILE):
        kj = kf[:, j * SEG_TILE:(j + 1) * SEG_TILE]
        k_ms = _dot((kj * kj).astype(BF16), segk_ref[...]) * (1.0 / QK_NOPE)
        kn = kj * lax.rsqrt(k_ms + EPS) * gkn_ref[...]
        for i in range(SEG_TILE // HEAD_PAD):
            k_ref[0, j * (SEG_TILE // HEAD_PAD) + i] = (kn[:, i * HEAD_PAD:(i + 1) * HEAD_PAD] + k_rope).astype(BF16)


def _mla_in(h, mod, norm_w, w_in, q_a_norm, w_uq_t, kv_a_norm, w_uk, w_uv_t, gq_t, gkn, gkr, seg_k,
            cos_t, sin_lo, sin_hi, cos_q, sin_q):
    head = jax.ShapeDtypeStruct((BATCH, MLA_HEADS, N_TOK, HEAD_PAD), BF16)
    head_spec = pl.BlockSpec((1, MLA_HEADS, TOK_BLOCK, HEAD_PAD), lambda b, t: (b, 0, t, 0))
    per_qb = Q_BLOCK // TOK_BLOCK
    q_lat = jax.ShapeDtypeStruct((BATCH, MLA_HEADS, SEQ // Q_BLOCK, HEAD_PAD, Q_BLOCK), BF16)

    def q_map(b, t):
        lat = jnp.maximum(t - CTX_BLOCKS, 0)
        return (b, 0, lat // per_qb, 0, lat % per_qb)
    q_spec = pl.BlockSpec((1, MLA_HEADS, 1, HEAD_PAD, TOK_BLOCK), q_map)
    vt = jax.ShapeDtypeStruct((BATCH, MLA_HEADS, N_TOK // KV_CHUNK, V_ROWS, KV_CHUNK), BF16)
    vt_spec = pl.BlockSpec((1, MLA_HEADS, 1, V_ROWS, KV_CHUNK), lambda b, t: (b, 0, t, 0, 0))
    tab_spec = pl.BlockSpec((TOK_BLOCK, HEAD_PAD), lambda b, t: (t, 0))
    tab_t_spec = pl.BlockSpec((QK_ROPE // 2, TOK_BLOCK), lambda b, t: (0, t))
    row = lambda n: pl.BlockSpec((1, n), _const2)
    full = lambda *shape: pl.BlockSpec(shape, _const2, pipeline_mode=pl.Buffered(1))
    return pl.pallas_call(
        _mla_in_kernel,
        out_shape=(q_lat, head, vt, jax.ShapeDtypeStruct((BATCH, N_TOK, MLA_MIX), F32)),
        grid=(BATCH, N_BLOCKS),
        in_specs=[
            pl.BlockSpec((1, TOK_BLOCK, D_MODEL), _tok_map),
            pl.BlockSpec((1, 3, D_MODEL), _mod_map),
            row(D_MODEL),
            full(D_MODEL, ODD_IN_PAD),
            row(Q_LORA),
            full(MLA_HEADS * HEAD_PAD, Q_LORA),
            row(KV_LORA),
            full(KV_LORA, MLA_HEADS * HEAD_PAD),
            full(MLA_MIX, KV_LORA),
            full(HEAD_PAD, TOK_BLOCK), row(SEG_TILE), row(HEAD_PAD),
            full(SEG_TILE, SEG_TILE),
            tab_spec, tab_spec, tab_spec, tab_t_spec, tab_t_spec,
        ],
        out_specs=(q_spec, head_spec, vt_spec, pl.BlockSpec((1, TOK_BLOCK, MLA_MIX), _tok_map)),
        compiler_params=_params("arbitrary", "arbitrary"),
        name="mla_in",
    )(h, mod, norm_w, w_in, q_a_norm, w_uq_t, kv_a_norm, w_uk, w_uv_t, gq_t, gkn, gkr, seg_k,
      cos_t, sin_lo, sin_hi, cos_q, sin_q)


def _attention_kernel(q_ref, k_ref, vt_ref, o_ref, s_ref, ot_ref):
    n_chunks = N_TOK // KV_CHUNK
    groups = KV_CHUNK // SUBLANES
    n_qb = SEQ // Q_BLOCK
    assert HEADS_PER_STEP % 2 == 0

    def q_rows(qb):
        return pl.ds(pl.multiple_of(qb * Q_BLOCK, Q_BLOCK), Q_BLOCK)

    def chunk(c):
        return slice(c * KV_CHUNK, (c + 1) * KV_CHUNK)

    def stages(score, value, mx_prev):
        mx = m_val = None
        if value is not None:
            m_val = jnp.max(mx_prev, axis=0, keepdims=True)
        for c in range(n_chunks):
            if score is not None:
                qb, hh = score
                sc = _dot(k_ref[0, hh, chunk(c), :], q_ref[0, hh, qb])
                s_ref[hh % 2, chunk(c), :] = sc
                cm = jnp.max(sc.reshape(groups, SUBLANES, Q_BLOCK), axis=0)
                mx = cm if mx is None else jnp.maximum(mx, cm)
            if value is not None:
                qb, hh = value
                e = jnp.exp2(s_ref[hh % 2, chunk(c), :] - m_val)
                pv = _dot(vt_ref[0, hh, c], e.astype(BF16))
                acc = pv if c == 0 else acc + pv
        if value is not None:
            qb, hh = value
            inv = 1.0 / acc[V_HEAD:V_HEAD + 1, :]
            ot_ref[hh * V_HEAD:(hh + 1) * V_HEAD, :] = acc[0:V_HEAD, :] * inv
            if hh == HEADS_PER_STEP - 1:
                o_ref[0, q_rows(qb), :] = ot_ref[...].T
        return mx

    def within_block(qb, mx):
        for hh in range(HEADS_PER_STEP - 1):
            mx = stages((qb, hh + 1), (qb, hh), mx)
        return mx

    def block(qb, mx):
        return stages((qb + 1, 0), (qb, HEADS_PER_STEP - 1), within_block(qb, mx))

    mx = stages((0, 0), None, None)
    mx = lax.fori_loop(0, n_qb - 1, block, mx)
    mx = within_block(n_qb - 1, mx)
    stages(None, (n_qb - 1, HEADS_PER_STEP - 1), mx)


def _attention(q, k, vt):
    pairs = MLA_HEADS // HEADS_PER_STEP
    pair_map = lambda b, hp: (b, hp, 0, 0)
    return pl.pallas_call(
        _attention_kernel,
        out_shape=jax.ShapeDtypeStruct((BATCH, SEQ, MLA_MIX), F32),
        grid=(BATCH, pairs),
        in_specs=[
            pl.BlockSpec((1, HEADS_PER_STEP, SEQ // Q_BLOCK, HEAD_PAD, Q_BLOCK), lambda b, hp: (b, hp, 0, 0, 0)),
            pl.BlockSpec((1, HEADS_PER_STEP, N_TOK, HEAD_PAD), pair_map),
            pl.BlockSpec((1, HEADS_PER_STEP, N_TOK // KV_CHUNK, V_ROWS, KV_CHUNK), lambda b, hp: (b, hp, 0, 0, 0)),
        ],
        out_specs=pl.BlockSpec((1, SEQ, HEADS_PER_STEP * V_HEAD), lambda b, hp: (b, 0, hp)),
        scratch_shapes=[
            pltpu.VMEM((2, N_TOK, Q_BLOCK), F32),
            pltpu.VMEM((HEADS_PER_STEP * V_HEAD, Q_BLOCK), F32),
        ],
        compiler_params=_params("arbitrary", "arbitrary"),
        name="attention",
    )(q, k, vt)


OUT_PARTS = 4


def _mla_out_kernel(att_ref, *refs):
    gate_refs, h_refs = refs[:OUT_PARTS], refs[OUT_PARTS:2 * OUT_PARTS]
    mod_ref, wo_ref, o_ref = refs[2 * OUT_PARTS:]
    for part in range(OUT_PARTS):
        rows = slice(part * TOK_BLOCK, (part + 1) * TOK_BLOCK)
        mix = (att_ref[0, rows, :] * gate_refs[part][0]).astype(BF16)
        o_ref[0, rows, :] = h_refs[part][0] + mod_ref[0, 2:3, :] * _dot(mix, wo_ref[...])


def _mla_out(att, gate, h, mod, w_out):
    lat_part = lambda part: (lambda b, t: (b, t * OUT_PARTS + part + CTX_BLOCKS, 0))
    wide = pl.BlockSpec((1, OUT_PARTS * TOK_BLOCK, D_MODEL), _tok_map)
    return pl.pallas_call(
        _mla_out_kernel,
        out_shape=jax.ShapeDtypeStruct((BATCH, SEQ, D_MODEL), F32),
        grid=(BATCH, SEQ // (OUT_PARTS * TOK_BLOCK)),
        in_specs=[wide]
        + [pl.BlockSpec((1, TOK_BLOCK, MLA_MIX), lat_part(part)) for part in range(OUT_PARTS)]
        + [pl.BlockSpec((1, TOK_BLOCK, D_MODEL), lat_part(part)) for part in range(OUT_PARTS)]
        + [pl.BlockSpec((1, 3, D_MODEL), lambda b, t: (b, 0, 0)),
           pl.BlockSpec((MLA_MIX, D_MODEL), _const2, pipeline_mode=pl.Buffered(1))],
        out_specs=wide,
        compiler_params=_params("arbitrary", "arbitrary"),
        name="mla_out",
    )(att, *([gate] * OUT_PARTS), *([h] * OUT_PARTS), mod, w_out)


def _pad_heads(w, width, offset=0):
    k = w.shape[0]
    w = w.reshape(k, MLA_HEADS, width)
    return jnp.pad(w, ((0, 0), (0, 0), (offset, HEAD_PAD - width - offset))).reshape(k, MLA_HEADS * HEAD_PAD)


def _mla_weights(w_in, w_uq, w_ukv, q_norm, k_norm):
    cuts = [Q_LORA, Q_LORA + KV_LORA, Q_LORA + KV_LORA + QK_ROPE]
    w_kr = jnp.pad(w_in[:, cuts[1]:cuts[2]], ((0, 0), (QK_NOPE, LANES - QK_DIM)))
    w_in_p = jnp.concatenate([w_in[:, :cuts[1]], w_kr, w_in[:, cuts[2]:]], axis=1).astype(BF16)
    w_uq_t = _pad_heads(w_uq, QK_DIM).T.astype(BF16)
    w_kv = w_ukv.reshape(KV_LORA, MLA_HEADS, QK_NOPE + V_HEAD)
    w_uk = _pad_heads(w_kv[:, :, :QK_NOPE].reshape(KV_LORA, -1), QK_NOPE).astype(BF16)
    w_uv_t = w_kv[:, :, QK_NOPE:].reshape(KV_LORA, MLA_MIX).T.astype(BF16)
    pad_row = lambda v, off: jnp.pad(v, (off, HEAD_PAD - v.shape[0] - off)).reshape(1, HEAD_PAD)
    gq_t = jnp.broadcast_to(pad_row(q_norm, 0).reshape(HEAD_PAD, 1), (HEAD_PAD, TOK_BLOCK))
    gkn = jnp.tile(pad_row(k_norm[:QK_NOPE], 0), (1, SEG_TILE // HEAD_PAD))
    gkr = pad_row(k_norm[QK_NOPE:], QK_NOPE)
    return w_in_p, w_uq_t, w_uk, w_uv_t, gq_t, gkn, gkr


def _key_segment_matrix():
    lane = np.arange(SEG_TILE)
    is_nope = lane % HEAD_PAD < QK_NOPE
    same_head = (lane // HEAD_PAD)[:, None] == (lane // HEAD_PAD)[None, :]
    return jnp.asarray(same_head & is_nope[:, None] & is_nope[None, :], dtype=BF16)


def _rope_tables():
    f32 = np.float32
    rows = SEQ // GRID_W
    row = np.repeat(np.arange(rows, dtype=f32), GRID_W)
    col = np.tile(np.arange(GRID_W, dtype=f32), rows)
    n_freq = QK_ROPE // 4
    inv = np.power(f32(ROPE_BASE), -np.arange(n_freq, dtype=f32) / f32(n_freq)).astype(f32)
    ang = np.concatenate([row[:, None] * inv, col[:, None] * inv], axis=-1).astype(f32)
    cos, sin = np.cos(ang).astype(f32), np.sin(ang).astype(f32)
    half = QK_ROPE // 2
    zeros = np.zeros_like(cos)
    pad_l = np.ones((SEQ, QK_NOPE), f32)
    tail = np.zeros((SEQ, HEAD_PAD - QK_DIM), f32)
    cos_t = np.concatenate([pad_l, cos, cos, tail], axis=-1)
    sin_lo = np.concatenate([0 * pad_l, -sin, zeros, tail], axis=-1)
    sin_hi = np.concatenate([0 * pad_l, zeros, sin, tail], axis=-1)
    ctx_cos = np.concatenate([np.ones((CTX_LEN, QK_DIM), f32), np.zeros((CTX_LEN, HEAD_PAD - QK_DIM), f32)], -1)
    ctx_zero = np.zeros((CTX_LEN, HEAD_PAD), f32)
    cos_q = np.concatenate([np.ones((half, CTX_LEN), f32), cos.T], axis=1)
    sin_q = np.concatenate([np.zeros((half, CTX_LEN), f32), sin.T], axis=1)
    tables = (np.concatenate([ctx_cos, cos_t], 0), np.concatenate([ctx_zero, sin_lo], 0),
              np.concatenate([ctx_zero, sin_hi], 0), cos_q, sin_q)
    return tuple(jnp.asarray(t) for t in tables)


def kernel(x, c, ctx, c_ctx, mod_w, mod_b, norm_w, e_w_in, e_conv_w, e_lam_re, e_lam_im, e_log_step, e_b_re, e_b_im, e_c_re, e_c_im, e_d, e_glu_w, e_glu_b, e_w_out, o_w_in, o_q_a_norm, o_w_uq, o_kv_a_norm, o_w_ukv, o_q_norm, o_k_norm, o_w_out):
    cond = jnp.zeros((COND_ROWS, D_MODEL), F32).at[:BATCH].set(c).at[CTX_ROW].set(c_ctx)
    mod = _modulation(cond, mod_w, mod_b)[:, :CTX_ROW + 1].reshape(2, CTX_ROW + 1, 3, D_MODEL)

    mod_even = jnp.stack([mod[0, :BATCH], jnp.broadcast_to(mod[0, CTX_ROW], (BATCH, 3, D_MODEL))])
    perm, perm_t = _time_major_perm()
    p, q, zs, u, u_tm = _even_in(ctx, x, mod_even, norm_w[0:1], e_w_in[0].astype(BF16), perm)
    a_re, a_im, bbr, bbi = _s5_discretize(e_lam_re[0], e_lam_im[0], e_log_step[0], e_b_re[0], e_b_im[0])
    w_b, c_re, c_im = _s5_matrices(bbr, bbi, e_c_re[0], e_c_im[0])
    bcast = lambda a: jnp.broadcast_to(a[:, None, :], (2, BATCH, N_STATE))
    scan_args = (u_tm, w_b, c_re, c_im, bcast(a_re), bcast(a_im))
    y = _s5_scan(1, *scan_args, y_prev=_s5_scan(0, *scan_args))
    h1 = _even_out(y, u, p, q, zs, ctx, x, mod_even, e_d[0:1], e_glu_w[0].astype(BF16), e_glu_b[0:1],
                   e_conv_w[0], e_w_out[0].astype(BF16), perm_t)

    w_in_p, w_uq_t, w_uk, w_uv_t, gq_t, gkn, gkr = _mla_weights(o_w_in[0], o_w_uq[0], o_w_ukv[0],
                                                                o_q_norm[0], o_k_norm[0])
    cos_t, sin_lo, sin_hi, cos_q, sin_q = _rope_tables()
    qh, kh, vh, gate = _mla_in(h1, mod[1], norm_w[1:2], w_in_p, o_q_a_norm[0:1], w_uq_t, o_kv_a_norm[0:1],
                               w_uk, w_uv_t, gq_t, gkn, gkr, _key_segment_matrix(),
                               cos_t, sin_lo, sin_hi, cos_q, sin_q)
    att = _attention(qh, kh, vh)
    return _mla_out(att, gate, h1, mod[1], o_w_out[0].astype(BF16))
```
